```python
import math, functools
import jax, jax.numpy as jnp
from jax import lax
import numpy as np

D_MODEL = 1024
BATCH = 4
SEQ = 8192
DEPTH = 1
DEC_BATCH = 1
DEC_SEQ = 16384
PAST_LEN = 128

N_META = 16
GRID_W = 64
EPS = 1e-6

A_QK_HEADS = 8
A_V_HEADS = 16
A_HEAD_K = 128
A_HEAD_V = 128
A_KEY_DIM = A_QK_HEADS * A_HEAD_K
A_VAL_DIM = A_V_HEADS * A_HEAD_V
CONV_W = 5
CONV_CH = 2 * A_KEY_DIM + A_VAL_DIM
CHUNK = 64
PAD_A = (CHUNK - N_META % CHUNK) % CHUNK

B_Q_HEADS = 8
B_KV_HEADS = 2
B_HEAD = 128
B_Q_DIM = B_Q_HEADS * B_HEAD
B_KV_DIM = B_KV_HEADS * B_HEAD
Q_BLOCK = 128
PAD_B = (Q_BLOCK - N_META % Q_BLOCK) % Q_BLOCK
ROPE_THETA = 10000.0

N_GROUPS = 4
EXPERTS_PER_GROUP = 8
N_EXPERTS = N_GROUPS * EXPERTS_PER_GROUP
TOP_K = 2
D_FF_EXPERT = 256

SPLITS = (A_KEY_DIM, A_KEY_DIM, A_VAL_DIM, A_VAL_DIM, 4 * A_V_HEADS, B_Q_DIM, B_KV_DIM, B_KV_DIM, D_MODEL, D_MODEL)
D_IN_PROJ = 2 * A_KEY_DIM + 2 * A_VAL_DIM + 4 * A_V_HEADS + B_Q_DIM + 2 * B_KV_DIM + 2 * D_MODEL

kernel_name = "hybrid_deltanet_gqa_hmoe_encoder"


def rmsnorm(x, gain):
    xf = x.astype(jnp.float32)
    y = xf * lax.rsqrt(jnp.mean(xf * xf, axis=-1, keepdims=True) + EPS)
    return (y * gain.astype(jnp.float32)).astype(x.dtype)


def l2norm(x):
    xf = x.astype(jnp.float32)
    return xf * lax.rsqrt(jnp.sum(xf * xf, axis=-1, keepdims=True) + EPS)


def centred_short_conv(x, w):
    y = lax.conv_general_dilated(
        x, w[:, None, :].astype(x.dtype), window_strides=(1,),
        padding=[(CONV_W // 2, CONV_W // 2)],
        dimension_numbers=('NWC', 'WIO', 'NWC'), feature_group_count=x.shape[-1])
    return jax.nn.silu(y)


def chunked_gated_delta(q, k, v, beta, g):
    n_b, t_len, n_h, dk = q.shape
    dv = v.shape[-1]
    n_c = t_len // CHUNK

    def chunks(a):
        a = a.reshape((n_b, n_c, CHUNK, n_h) + a.shape[3:])
        return jnp.swapaxes(jnp.moveaxis(a, 1, 0), 2, 3)

    xs = (chunks(q * (dk ** -0.5)), chunks(k), chunks(v), chunks(beta), chunks(g))
    incl = jnp.tril(jnp.ones((CHUNK, CHUNK), dtype=bool))
    strict = jnp.tril(jnp.ones((CHUNK, CHUNK), dtype=bool), -1)
    eye = jnp.eye(CHUNK, dtype=jnp.float32)

    def step(state, inp):
        q_c, k_c, v_c, b_c, g_c = inp
        gc = jnp.cumsum(g_c, axis=-1)
        diff = gc[..., :, None] - gc[..., None, :]
        decay = jnp.where(incl, jnp.exp(jnp.where(incl, diff, 0.0)), 0.0)
        k_beta = k_c * b_c[..., None]
        a_mat = jnp.where(strict, jnp.einsum('nhcd,nhsd->nhcs', k_beta, k_c) * decay, 0.0)
        t_mat = lax.linalg.triangular_solve(
            eye + a_mat, jnp.broadcast_to(eye, a_mat.shape),
            left_side=True, lower=True, unit_diagonal=True)
        u = jnp.einsum('nhcs,nhse->nhce', t_mat, v_c * b_c[..., None])
        w = jnp.einsum('nhcs,nhsd->nhcd', t_mat, k_beta * jnp.exp(gc)[..., None])
        v_new = u - jnp.einsum('nhcd,nhde->nhce', w, state)
        qk = jnp.where(incl, jnp.einsum('nhcd,nhsd->nhcs', q_c, k_c) * decay, 0.0)
        o_c = (jnp.einsum('nhcd,nhde->nhce', q_c * jnp.exp(gc)[..., None], state)
               + jnp.einsum('nhcs,nhse->nhce', qk, v_new))
        g_last = gc[..., -1]
        k_dec = k_c * jnp.exp(g_last[..., None] - gc)[..., None]
        state = state * jnp.exp(g_last)[..., None, None] + jnp.einsum('nhcd,nhce->nhde', k_dec, v_new)
        return state, o_c

    s0 = jnp.zeros((n_b, n_h, dk, dv), jnp.float32)
    _, o = lax.scan(step, s0, xs)
    o = jnp.moveaxis(jnp.swapaxes(o, 2, 3), 0, 1)
    return o.reshape(n_b, t_len, n_h, dv)


def gated_deltanet_branch(qa, ka, va, za, ab, conv_w, a_log_fwd, a_log_bwd,
                          dt_bias_fwd, dt_bias_bwd, norm_out_a):
    n_b, n_tok, _ = qa.shape
    f32 = jnp.float32
    qkv = centred_short_conv(jnp.concatenate([qa, ka, va], axis=-1), conv_w).astype(f32)
    q, k, v = jnp.split(qkv, [A_KEY_DIM, 2 * A_KEY_DIM], axis=-1)
    rep = A_V_HEADS // A_QK_HEADS
    q = jnp.repeat(l2norm(q.reshape(n_b, n_tok, A_QK_HEADS, A_HEAD_K)), rep, axis=2)
    k = jnp.repeat(l2norm(k.reshape(n_b, n_tok, A_QK_HEADS, A_HEAD_K)), rep, axis=2)
    v = v.reshape(n_b, n_tok, A_V_HEADS, A_HEAD_V)
    a_f, a_b, b_f, b_b = jnp.split(ab.astype(f32), 4, axis=-1)
    g_f = -jnp.exp(a_log_fwd.astype(f32)) * jax.nn.softplus(a_f + dt_bias_fwd.astype(f32))
    g_b = -jnp.exp(a_log_bwd.astype(f32)) * jax.nn.softplus(a_b + dt_bias_bwd.astype(f32))
    beta_f = jax.nn.sigmoid(b_f)
    beta_b = jax.nn.sigmoid(b_b)

    def pad_front(t):
        return jnp.pad(t, ((0, 0), (PAD_A, 0)) + ((0, 0),) * (t.ndim - 2))

    def both(t_fwd, t_bwd):
        return jnp.concatenate([pad_front(t_fwd), jnp.flip(pad_front(t_bwd), axis=1)], axis=0)

    o = chunked_gated_delta(both(q, q), both(k, k), both(v, v),
                            both(beta_f, beta_b), both(g_f, g_b))
    o = (o[:n_b] + jnp.flip(o[n_b:], axis=1))[:, PAD_A:]
    z = za.reshape(n_b, n_tok, A_V_HEADS, A_HEAD_V).astype(f32)
    o = rmsnorm(o, norm_out_a) * jax.nn.silu(z)
    return o.reshape(n_b, n_tok, A_VAL_DIM)


def axial_rope_angles(n_real):
    rows_n = n_real // GRID_W
    f32 = jnp.float32
    row = jnp.concatenate([jnp.full((N_META,), -1.0, f32), jnp.repeat(jnp.arange(rows_n, dtype=f32), GRID_W)])
    col = jnp.concatenate([jnp.arange(N_META, dtype=f32), jnp.tile(jnp.arange(GRID_W, dtype=f32), rows_n)])
    half = B_HEAD // 2
    inv_freq = ROPE_THETA ** (-jnp.arange(0, half, 2, dtype=f32) / half)
    return row[:, None] * inv_freq, col[:, None] * inv_freq


def rotate(x, ang):
    m = ang.shape[-1]
    cos = jnp.cos(ang)[:, None, :]
    sin = jnp.sin(ang)[:, None, :]
    x1, x2 = x[..., :m], x[..., m:]
    return jnp.concatenate([x1 * cos - x2 * sin, x1 * sin + x2 * cos], axis=-1)


def axial_rope(x, ang_row, ang_col):
    half = B_HEAD // 2
    return jnp.concatenate([rotate(x[..., :half], ang_row), rotate(x[..., half:], ang_col)], axis=-1)


def gqa_branch(qb, kb, vb, norm_q_b, norm_k_b, n_real):
    n_b, n_tok, _ = qb.shape
    f32 = jnp.float32
    q = rmsnorm(qb.reshape(n_b, n_tok, B_Q_HEADS, B_HEAD).astype(f32), norm_q_b)
    k = rmsnorm(kb.reshape(n_b, n_tok, B_KV_HEADS, B_HEAD).astype(f32), norm_k_b)
    v = vb.reshape(n_b, n_tok, B_KV_HEADS, B_HEAD).astype(f32)
    ang_r, ang_c = axial_rope_angles(n_real)
    q = axial_rope(q, ang_r, ang_c)
    k = axial_rope(k, ang_r, ang_c)
    groups = B_Q_HEADS // B_KV_HEADS
    q = jnp.pad(q, ((0, 0), (PAD_B, 0), (0, 0), (0, 0)))
    n_blk = (n_tok + PAD_B) // Q_BLOCK
    q = jnp.moveaxis(q.reshape(n_b, n_blk, Q_BLOCK, B_KV_HEADS, groups, B_HEAD), 1, 0)
    scale = B_HEAD ** -0.5

    def attend(q_blk):
        s = jnp.einsum('bqkgd,bskd->bkgqs', q_blk, k) * scale
        p = jax.nn.softmax(s, axis=-1)
        return jnp.einsum('bkgqs,bskd->bqkgd', p, v)

    o = lax.map(attend, q)
    o = jnp.moveaxis(o, 0, 1).reshape(n_b, n_tok + PAD_B, B_Q_DIM)
    return o[:, PAD_B:]


def hierarchical_moe(x, w_group, b_group, w_router, b_router, w_gate_e, w_up_e, w_down_e):
    n_t = x.shape[0]
    f32 = jnp.float32
    grp_prob = jax.nn.softmax((x @ w_group).astype(f32) + b_group.astype(f32), axis=-1)
    g_idx = jnp.argmax(grp_prob, axis=-1)
    g_w = jnp.take_along_axis(grp_prob, g_idx[:, None], axis=-1)
    e_logits = ((x @ w_router).astype(f32) + b_router.astype(f32)).reshape(n_t, N_GROUPS, EXPERTS_PER_GROUP)
    in_grp = jnp.take_along_axis(e_logits, g_idx[:, None, None], axis=1)[:, 0]
    top_v, top_i = lax.top_k(in_grp, TOP_K)
    top_w = jax.nn.softmax(top_v, axis=-1) * g_w
    local = jnp.sum(jax.nn.one_hot(top_i, EXPERTS_PER_GROUP, dtype=f32) * top_w[..., None], axis=1)
    combine = (jax.nn.one_hot(g_idx, N_GROUPS, dtype=f32)[:, :, None] * local[:, None, :]).reshape(n_t, N_EXPERTS)
    y = jnp.zeros((n_t, x.shape[-1]), f32)
    for e in range(N_EXPERTS):
        h = jax.nn.silu(x @ w_gate_e[e]) * (x @ w_up_e[e])
        y = y + combine[:, e:e + 1] * (h @ w_down_e[e]).astype(f32)
    return y.astype(x.dtype)


def encoder_forward(x, meta_tokens, norm_mix, w_in, conv_w, a_log_fwd, a_log_bwd, dt_bias_fwd,
                    dt_bias_bwd, norm_out_a, norm_q_b, norm_k_b, w_branch_a, w_branch_b, w_out,
                    norm_ffn, w_group, b_group, w_router, b_router, w_gate_e, w_up_e, w_down_e,
                    norm_final):
    n_b, n_real, d = x.shape
    h = jnp.concatenate([jnp.broadcast_to(meta_tokens.astype(x.dtype)[None], (n_b, N_META, d)), x], axis=1)
    n_tok = n_real + N_META
    split_at = np.cumsum(np.array(SPLITS))[:-1].tolist()
    for l in range(DEPTH):
        u = rmsnorm(h, norm_mix[l])
        proj = u @ w_in[l]
        qa, ka, va, za, ab, qb, kb, vb, gate_a, gate_b = jnp.split(proj, split_at, axis=-1)
        ya = gated_deltanet_branch(qa, ka, va, za, ab, conv_w[l], a_log_fwd[l], a_log_bwd[l],
                                   dt_bias_fwd[l], dt_bias_bwd[l], norm_out_a[l])
        yb = gqa_branch(qb, kb, vb, norm_q_b[l], norm_k_b[l], n_real)
        ya = ya.astype(h.dtype) @ w_branch_a[l]
        yb = yb.astype(h.dtype) @ w_branch_b[l]
        merged = jax.nn.sigmoid(gate_a) * ya + jax.nn.sigmoid(gate_b) * yb
        h = h + (merged @ w_out[l]).astype(h.dtype)
        m = hierarchical_moe(rmsnorm(h, norm_ffn[l]).reshape(n_b * n_tok, d), w_group[l], b_group[l],
                             w_router[l], b_router[l], w_gate_e[l], w_up_e[l], w_down_e[l])
        h = h + m.reshape(n_b, n_tok, d).astype(h.dtype)
    return rmsnorm(h, norm_final)[:, N_META:]


def setup_inputs(seed: int = 0) -> dict:
    key = jax.random.key(seed)
    ks = jax.random.split(key, 32)
    f32 = jnp.float32

    def nrm(k, shape, scale):
        return jax.random.normal(k, shape, f32) * scale

    def gain(k, shape):
        return 1.0 + 0.02 * jax.random.normal(k, shape, f32)

    def dt_bias(k):
        dt = jnp.exp(jax.random.uniform(k, (DEPTH, A_V_HEADS), f32, math.log(1e-3), math.log(1e-1)))
        return dt + jnp.log(-jnp.expm1(-dt))

    def a_log(k):
        return jnp.log(jax.random.uniform(k, (DEPTH, A_V_HEADS), f32, 1.0, 16.0))

    return {
        "x_prompt": nrm(ks[0], (BATCH, SEQ, D_MODEL), 1.0),
        "x_sample": nrm(ks[1], (DEC_BATCH, DEC_SEQ, D_MODEL), 1.0),
        "meta_tokens": nrm(ks[2], (N_META, D_MODEL), 1.0),
        "norm_mix": gain(ks[3], (DEPTH, D_MODEL)),
        "w_in": nrm(ks[4], (DEPTH, D_MODEL, D_IN_PROJ), D_MODEL ** -0.5),
        "conv_w": nrm(ks[5], (DEPTH, CONV_W, CONV_CH), CONV_W ** -0.5),
        "a_log_fwd": a_log(ks[6]),
        "a_log_bwd": a_log(ks[7]),
        "dt_bias_fwd": dt_bias(ks[8]),
        "dt_bias_bwd": dt_bias(ks[9]),
        "norm_out_a": gain(ks[10], (DEPTH, A_HEAD_V)),
        "norm_q_b": gain(ks[11], (DEPTH, B_HEAD)),
        "norm_k_b": gain(ks[12], (DEPTH, B_HEAD)),
        "w_branch_a": nrm(ks[13], (DEPTH, A_VAL_DIM, D_MODEL), A_VAL_DIM ** -0.5),
        "w_branch_b": nrm(ks[14], (DEPTH, B_Q_DIM, D_MODEL), B_Q_DIM ** -0.5),
        "w_out": nrm(ks[15], (DEPTH, D_MODEL, D_MODEL), D_MODEL ** -0.5),
        "norm_ffn": gain(ks[16], (DEPTH, D_MODEL)),
        "w_group": nrm(ks[17], (DEPTH, D_MODEL, N_GROUPS), D_MODEL ** -0.5),
        "b_group": nrm(ks[18], (DEPTH, N_GROUPS), 0.01),
        "w_router": nrm(ks[19], (DEPTH, D_MODEL, N_EXPERTS), D_MODEL ** -0.5),
        "b_router": nrm(ks[20], (DEPTH, N_EXPERTS), 0.01),
        "w_gate_e": nrm(ks[21], (DEPTH, N_EXPERTS, D_MODEL, D_FF_EXPERT), D_MODEL ** -0.5),
        "w_up_e": nrm(ks[22], (DEPTH, N_EXPERTS, D_MODEL, D_FF_EXPERT), D_MODEL ** -0.5),
        "w_down_e": nrm(ks[23], (DEPTH, N_EXPERTS, D_FF_EXPERT, D_MODEL), D_FF_EXPERT ** -0.5),
        "norm_final": gain(ks[24], (D_MODEL,)),
    }


def reference(x_prompt, x_sample, meta_tokens, norm_mix, w_in, conv_w, a_log_fwd, a_log_bwd,
              dt_bias_fwd, dt_bias_bwd, norm_out_a, norm_q_b, norm_k_b, w_branch_a, w_branch_b,
              w_out, norm_ffn, w_group, b_group, w_router, b_router, w_gate_e, w_up_e, w_down_e,
              norm_final):
    forward = functools.partial(
        encoder_forward, meta_tokens=meta_tokens, norm_mix=norm_mix, w_in=w_in, conv_w=conv_w,
        a_log_fwd=a_log_fwd, a_log_bwd=a_log_bwd, dt_bias_fwd=dt_bias_fwd, dt_bias_bwd=dt_bias_bwd,
        norm_out_a=norm_out_a, norm_q_b=norm_q_b, norm_k_b=norm_k_b, w_branch_a=w_branch_a,
        w_branch_b=w_branch_b, w_out=w_out, norm_ffn=norm_ffn, w_group=w_group, b_group=b_group,
        w_router=w_router, b_router=b_router, w_gate_e=w_gate_e, w_up_e=w_up_e, w_down_e=w_down_e,
        norm_final=norm_final)
    y_prompt = forward(x_prompt)
    y_sample = forward(x_sample)
    return (y_prompt, y_sample)
```

```python
import functools
import math

import numpy as np
import jax
import jax.numpy as jnp
from jax import lax
from jax.experimental import pallas as pl
from jax.experimental.pallas import tpu as pltpu

F32 = jnp.float32
BF16 = jnp.bfloat16

D_MODEL = 1024
N_META = 16
GRID_W = 64
EPS = 1e-6
HEAD = 128
A_QK_HEADS = 8
A_V_HEADS = 16
A_KEY_DIM = A_QK_HEADS * HEAD
A_VAL_DIM = A_V_HEADS * HEAD
CONV_W = 5
CONV_CH = 2 * A_KEY_DIM + A_VAL_DIM
CHUNK = 64
PAD_A = CHUNK - N_META
B_Q_HEADS = 8
B_KV_HEADS = 2
B_GROUPS = B_Q_HEADS // B_KV_HEADS
B_Q_DIM = B_Q_HEADS * HEAD
B_KV_DIM = B_KV_HEADS * HEAD
ROPE_THETA = 10000.0
N_GROUPS = 4
EXPERTS_PER_GROUP = 8
N_EXPERTS = N_GROUPS * EXPERTS_PER_GROUP
D_FF_EXPERT = 256

OFF_QKV = 0
OFF_Z = 4096
OFF_QB = 6144
OFF_GA = 7168
OFF_GB = 8192
OFF_KB = 9216
OFF_VB = 9472
N_MAIN = 9728
LANES = 128
GATE_LANES = 4 * A_V_HEADS
NEG_BIG = -1e30


def _gate_source_columns():
    src = []
    for lane in range(GATE_LANES):
        hq, kind, d, jj = lane // 8, (lane % 8) // 4, (lane % 4) // 2, lane % 2
        src.append(2 * A_V_HEADS * kind + A_V_HEADS * d + 2 * hq + jj)
    return np.asarray(src)
VMEM_LIMIT = 56 * 1024 * 1024


def _cparams(sem):
    return pltpu.CompilerParams(dimension_semantics=sem, vmem_limit_bytes=VMEM_LIMIT)


def _dot(a, b):
    return jnp.dot(a, b, preferred_element_type=F32)


def _dot_nt(a, b):
    return lax.dot_general(a, b, (((1,), (1,)), ((), ())), preferred_element_type=F32)


def _dot_tn(a, b):
    return lax.dot_general(a, b, (((0,), (0,)), ((), ())), preferred_element_type=F32)


def _dot_f32(a, b):
    return jnp.dot(a, b, preferred_element_type=F32, precision=lax.Precision.HIGHEST)


def _tile(n, pref):
    t = min(n, pref)
    assert n % t == 0, (n, t)
    return t


def _inproj_kernel(x_ref, g_ref, w_ref, wab_ref, o_ref, oab_ref, xn_ref):
    @pl.when(pl.program_id(1) == 0)
    def _():
        x = x_ref[...]
        xn = x * lax.rsqrt(jnp.mean(x * x, axis=-1, keepdims=True) + EPS) * g_ref[...]
        xb = xn.astype(BF16)
        xn_ref[...] = xb
        oab_ref[...] = _dot(xb, wab_ref[...])

    o_ref[...] = _dot(xn_ref[...], w_ref[...]).astype(o_ref.dtype)


def _inproj(x2, gain, w_main, w_ab):
    t = x2.shape[0]
    tm = _tile(t, 1024)
    tn = 512
    return pl.pallas_call(
        _inproj_kernel,
        grid=(t // tm, N_MAIN // tn),
        in_specs=[
            pl.BlockSpec((tm, D_MODEL), lambda i, j: (i, 0)),
            pl.BlockSpec((1, D_MODEL), lambda i, j: (0, 0)),
            pl.BlockSpec((D_MODEL, tn), lambda i, j: (0, j)),
            pl.BlockSpec((D_MODEL, LANES), lambda i, j: (0, 0)),
        ],
        out_specs=[
            pl.BlockSpec((tm, tn), lambda i, j: (i, j)),
            pl.BlockSpec((tm, LANES), lambda i, j: (i, 0)),
        ],
        out_shape=[jax.ShapeDtypeStruct((t, N_MAIN), BF16), jax.ShapeDtypeStruct((t, LANES), F32)],
        scratch_shapes=[pltpu.VMEM((tm, D_MODEL), BF16)],
        compiler_params=_cparams(("parallel", "arbitrary")),
        name="in_proj",
    )(x2, gain, w_main, w_ab)


def _rope_tables(row, col):
    half = HEAD // 2
    inv_freq = ROPE_THETA ** (-jnp.arange(0, half, 2, dtype=F32) / half)
    ang_r = row[:, None] * inv_freq
    ang_c = col[:, None] * inv_freq
    cos = jnp.concatenate([jnp.cos(ang_r)] * 2 + [jnp.cos(ang_c)] * 2, axis=-1)
    sin = jnp.concatenate([-jnp.sin(ang_r), jnp.sin(ang_r), -jnp.sin(ang_c), jnp.sin(ang_c)], axis=-1)
    return cos, sin


def _attn_prep_kernel(q_ref, k_ref, cos_ref, sin_ref, gq_ref, gk_ref, qo_ref, ko_ref):
    cos = cos_ref[...]
    sin = sin_ref[...]
    lane = lax.broadcasted_iota(jnp.int32, cos.shape, 1)
    low = (lane % (HEAD // 2)) < (HEAD // 4)

    def prep(x, gain, scale):
        x = x.astype(F32)
        xn = x * lax.rsqrt(jnp.mean(x * x, axis=-1, keepdims=True) + EPS) * gain
        partner = jnp.where(low, pltpu.roll(xn, HEAD - HEAD // 4, 1), pltpu.roll(xn, HEAD // 4, 1))
        return (xn * cos + partner * sin) * scale

    for h in range(B_Q_HEADS):
        sl = slice(h * HEAD, (h + 1) * HEAD)
        qo_ref[:, sl] = prep(q_ref[:, sl], gq_ref[...], HEAD ** -0.5).astype(qo_ref.dtype)
    for h in range(B_KV_HEADS):
        sl = slice(h * HEAD, (h + 1) * HEAD)
        ko_ref[:, sl] = prep(k_ref[:, sl], gk_ref[...], 1.0).astype(ko_ref.dtype)


def _attn_prep(proj, cos, sin, gq, gk, seq_len):
    t = proj.shape[0]
    tm = _tile(seq_len, 512)
    per_seq = seq_len // tm
    return pl.pallas_call(
        _attn_prep_kernel,
        grid=(t // tm,),
        in_specs=[
            pl.BlockSpec((tm, B_Q_DIM), lambda i: (i, OFF_QB // B_Q_DIM)),
            pl.BlockSpec((tm, B_KV_DIM), lambda i: (i, OFF_KB // B_KV_DIM)),
            pl.BlockSpec((tm, HEAD), lambda i: (i % per_seq, 0)),
            pl.BlockSpec((tm, HEAD), lambda i: (i % per_seq, 0)),
            pl.BlockSpec((1, HEAD), lambda i: (0, 0)),
            pl.BlockSpec((1, HEAD), lambda i: (0, 0)),
        ],
        out_specs=[
            pl.BlockSpec((tm, B_Q_DIM), lambda i: (i, 0)),
            pl.BlockSpec((tm, B_KV_DIM), lambda i: (i, 0)),
        ],
        out_shape=[jax.ShapeDtypeStruct((t, B_Q_DIM), BF16), jax.ShapeDtypeStruct((t, B_KV_DIM), BF16)],
        compiler_params=_cparams(("parallel",)),
        name="attn_prep",
    )(proj, proj, cos, sin, gq, gk)


def _flash_kernel(q_ref, k_ref, v_ref, km_ref, vm_ref, bias_ref, o_ref, m_ref, l_ref, acc_ref, *, tk, nkb):
    for g in range(B_GROUPS):
        q = q_ref[:, g * HEAD:(g + 1) * HEAD]
        s = _dot_nt(q, km_ref[...]) + bias_ref[...]
        m = jnp.max(s, axis=-1, keepdims=True)
        p = jnp.exp(s - m)
        m_ref[g] = m
        l_ref[g] = jnp.sum(p, axis=-1, keepdims=True)
        acc_ref[g] = _dot(p.astype(BF16), vm_ref[...])

    def body(kb, carry):
        start = pl.multiple_of(kb * tk, tk)
        kblk = k_ref[pl.ds(start, tk), :]
        vblk = v_ref[pl.ds(start, tk), :]
        for g in range(B_GROUPS):
            q = q_ref[:, g * HEAD:(g + 1) * HEAD]
            s = _dot_nt(q, kblk)
            m_old = m_ref[g]
            m_new = jnp.maximum(m_old, jnp.max(s, axis=-1, keepdims=True))
            alpha = jnp.exp(m_old - m_new)
            p = jnp.exp(s - m_new)
            l_ref[g] = alpha * l_ref[g] + jnp.sum(p, axis=-1, keepdims=True)
            acc_ref[g] = alpha * acc_ref[g] + _dot(p.astype(BF16), vblk)
            m_ref[g] = m_new
        return carry

    lax.fori_loop(0, nkb, body, 0)
    for g in range(B_GROUPS):
        o_ref[:, g * HEAD:(g + 1) * HEAD] = (acc_ref[g] / l_ref[g]).astype(o_ref.dtype)


def _flash(q_r, k_r, proj, km, vm, bias, n_seq, seq_len):
    tq = _tile(seq_len, 256)
    tk = _tile(seq_len, 512)
    q3 = q_r.reshape(n_seq, seq_len, B_Q_DIM)
    k3 = k_r.reshape(n_seq, seq_len, B_KV_DIM)
    p3 = proj.reshape(n_seq, seq_len, N_MAIN)
    gw = B_GROUPS * HEAD
    out = pl.pallas_call(
        functools.partial(_flash_kernel, tk=tk, nkb=seq_len // tk),
        grid=(n_seq, B_KV_HEADS, seq_len // tq),
        in_specs=[
            pl.BlockSpec((None, tq, gw), lambda b, j, i: (b, i, j)),
            pl.BlockSpec((None, seq_len, HEAD), lambda b, j, i: (b, 0, j)),
            pl.BlockSpec((None, seq_len, HEAD), lambda b, j, i: (b, 0, OFF_VB // HEAD + j)),
            pl.BlockSpec((LANES, HEAD), lambda b, j, i: (0, j)),
            pl.BlockSpec((LANES, HEAD), lambda b, j, i: (0, j)),
            pl.BlockSpec((1, LANES), lambda b, j, i: (0, 0)),
        ],
        out_specs=pl.BlockSpec((None, tq, gw), lambda b, j, i: (b, i, j)),
        out_shape=jax.ShapeDtypeStruct((n_seq, seq_len, B_Q_DIM), BF16),
        scratch_shapes=[
            pltpu.VMEM((B_GROUPS, tq, 1), F32),
            pltpu.VMEM((B_GROUPS, tq, 1), F32),
            pltpu.VMEM((B_GROUPS, tq, HEAD), F32),
        ],
        compiler_params=_cparams(("parallel", "parallel", "arbitrary")),
        name="flash_attn",
    )(q3, k3, p3, km, vm, bias)
    return out.reshape(n_seq * seq_len, B_Q_DIM)


CONV_CB = 512
HALO = 8


def _conv_kernel(x_ref, prev_ref, next_ref, ab_ref, cw_ref, alog_ref, dtb_ref,
                 q_ref, k_ref, v_ref, gc_ref, gt_ref, xs_ref, *, tm, pad_rows):
    row = lax.broadcasted_iota(jnp.int32, (tm, 1), 0)
    valid = (row % CHUNK) >= pad_rows
    for cb in range(CONV_CH // CONV_CB):
        cs = slice(cb * CONV_CB, (cb + 1) * CONV_CB)
        xs_ref[0:HALO, :] = prev_ref[0, :, cs].astype(F32)
        xs_ref[HALO:HALO + tm, :] = x_ref[:, cs].astype(F32)
        xs_ref[HALO + tm:, :] = next_ref[0, :, cs].astype(F32)
        y = jnp.zeros((tm, CONV_CB), F32)
        for w in range(CONV_W):
            y = y + xs_ref[pl.ds(HALO - CONV_W // 2 + w, tm), :] * cw_ref[w:w + 1, cs]
        y = y * (1.0 / (1.0 + jnp.exp(-y)))
        if pad_rows:
            y = jnp.where(valid, y, 0.0)
        for hh in range(CONV_CB // HEAD):
            col = cb * CONV_CB + hh * HEAD
            yh = y[:, hh * HEAD:(hh + 1) * HEAD]
            if col < 2 * A_KEY_DIM:
                yh = yh * lax.rsqrt(jnp.sum(yh * yh, axis=-1, keepdims=True) + EPS)
            if col < A_KEY_DIM:
                q_ref[col // HEAD] = yh * (HEAD ** -0.5)
            elif col < 2 * A_KEY_DIM:
                k_ref[(col - A_KEY_DIM) // HEAD] = yh
            else:
                v_ref[(col - 2 * A_KEY_DIM) // HEAD] = yh

    ab = ab_ref[...]
    lane = lax.broadcasted_iota(jnp.int32, (tm, LANES), 1)
    is_gate = lane < GATE_LANES
    is_a = is_gate & ((lane % 8) < 4)
    z = ab + dtb_ref[...]
    softplus = jnp.maximum(z, 0.0) + jnp.log1p(jnp.exp(-jnp.abs(z)))
    g = -jnp.exp(alog_ref[...]) * softplus
    beta = 1.0 / (1.0 + jnp.exp(-ab))
    g = jnp.where(valid & is_a, g, 0.0)
    beta = jnp.where(valid & is_gate & (~is_a), beta, 0.0)

    r = lax.broadcasted_iota(jnp.int32, (CHUNK, CHUNK), 0)
    c = lax.broadcasted_iota(jnp.int32, (CHUNK, CHUNK), 1)
    lower = (c <= r).astype(F32)
    upper = (c >= r).astype(F32)
    ones = jnp.ones((CHUNK, CHUNK), F32)
    lane_c = lax.broadcasted_iota(jnp.int32, (CHUNK, LANES), 1)
    fwd_lane = (lane_c % 4) < 2
    a_lane = (lane_c % 8) < 4
    for ck in range(tm // CHUNK):
        rs = slice(ck * CHUNK, (ck + 1) * CHUNK)
        gk = g[rs]
        gcum = jnp.where(fwd_lane, _dot_f32(lower, gk), _dot_f32(upper, gk))
        gtot = _dot_f32(ones, gk)
        e = jnp.exp(gcum)
        dec = jnp.exp(gtot - gcum)
        eg = jnp.exp(gtot)
        bk = beta[rs]
        t0 = jnp.where(a_lane, gcum, bk)
        t1 = jnp.where(a_lane, e, bk * pltpu.roll(e, 4, 1))
        t2 = jnp.where(a_lane, dec, pltpu.roll(eg, 4, 1))
        gc_ref[rs, 0:LANES] = t0
        gc_ref[rs, LANES:2 * LANES] = t1
        gc_ref[rs, 2 * LANES:3 * LANES] = t2
        gt_ref[0, 0:LANES, rs] = t0.T
        gt_ref[0, LANES:2 * LANES, rs] = t1.T
        gt_ref[0, 2 * LANES:3 * LANES, rs] = t2.T


def _conv(x, col_block, prev, nxt, ab, cw, alog, dtb, tm, pad_rows):
    t = x.shape[0]
    nt = t // tm
    return pl.pallas_call(
        functools.partial(_conv_kernel, tm=tm, pad_rows=pad_rows),
        grid=(nt,),
        in_specs=[
            pl.BlockSpec((tm, CONV_CH), lambda i: (i, col_block)),
            pl.BlockSpec((1, HALO, CONV_CH), lambda i: (i, 0, 0)),
            pl.BlockSpec((1, HALO, CONV_CH), lambda i: (i, 0, 0)),
            pl.BlockSpec((tm, LANES), lambda i: (i, 0)),
            pl.BlockSpec((HALO, CONV_CH), lambda i: (0, 0)),
            pl.BlockSpec((1, LANES), lambda i: (0, 0)),
            pl.BlockSpec((1, LANES), lambda i: (0, 0)),
        ],
        out_specs=[
            pl.BlockSpec((A_QK_HEADS, tm, HEAD), lambda i: (0, i, 0)),
            pl.BlockSpec((A_QK_HEADS, tm, HEAD), lambda i: (0, i, 0)),
            pl.BlockSpec((A_V_HEADS, tm, HEAD), lambda i: (0, i, 0)),
            pl.BlockSpec((tm, 3 * LANES), lambda i: (i, 0)),
            pl.BlockSpec((1, 3 * LANES, tm), lambda i: (i, 0, 0)),
        ],
        out_shape=[
            jax.ShapeDtypeStruct((A_QK_HEADS, t, HEAD), F32),
            jax.ShapeDtypeStruct((A_QK_HEADS, t, HEAD), F32),
            jax.ShapeDtypeStruct((A_V_HEADS, t, HEAD), F32),
            jax.ShapeDtypeStruct((t, 3 * LANES), F32),
            jax.ShapeDtypeStruct((nt, 3 * LANES, tm), F32),
        ],
        scratch_shapes=[pltpu.VMEM((tm + 2 * HALO, CONV_CB), F32)],
        compiler_params=_cparams(("parallel",)),
        name="conv_gates",
    )(x, prev, nxt, ab, cw, alog, dtb)


def _tri_inverse(a, r, c):
    def same(k):
        return (r >> k) == (c >> k)

    d = jnp.where(r == c, 1.0, 0.0) - jnp.where(same(1), a, 0.0)
    k = 1
    while (1 << k) < CHUNK:
        a_off = jnp.where(same(k + 1) & (~same(k)), a, 0.0)
        db = d.astype(BF16)
        d = d - _dot(db, _dot(a_off.astype(BF16), db).astype(BF16))
        k += 1
    return d


def _pre_kernel(q_ref, k_ref, gc_ref, gt_ref, tutw_ref, qk_ref, *, nck):
    hq = pl.program_id(1)
    r = lax.broadcasted_iota(jnp.int32, (CHUNK, CHUNK), 0)
    c = lax.broadcasted_iota(jnp.int32, (CHUNK, CHUNK), 1)
    g0 = pltpu.roll(gc_ref[:, 0:LANES], lax.rem(LANES - 8 * hq, LANES), 1)
    row0 = pl.multiple_of(8 * hq, 8)
    for ck in range(nck):
        rs = slice(ck * CHUNK, (ck + 1) * CHUNK)
        qb = q_ref[rs, :].astype(BF16)
        kb = k_ref[rs, :].astype(BF16)
        kk = _dot_nt(kb, kb)
        qk = _dot_nt(qb, kb)
        rows0 = gt_ref[pl.ds(row0, 8), rs]
        rows1 = gt_ref[pl.ds(LANES + row0, 8), rs]
        for d in range(2):
            incl = (c <= r) if d == 0 else (c >= r)
            strict = (c < r) if d == 0 else (c > r)
            qk_pair = []
            for jj in range(2):
                gl = 2 * d + jj
                gc_col = g0[rs, gl:gl + 1]
                beta_col = g0[rs, 4 + gl:5 + gl]
                gc_row = rows0[gl:gl + 1]
                beta_row = rows0[4 + gl:5 + gl]
                be_row = rows1[4 + gl:5 + gl]
                decay = jnp.where(incl, jnp.exp(jnp.where(incl, gc_col - gc_row, 0.0)), 0.0)
                a = jnp.where(strict, kk * decay, 0.0) * beta_col
                t = _tri_inverse(a, r, c)
                tutw = jnp.concatenate([t * beta_row, t * be_row], axis=-1)
                tutw_ref[d, jj, rs, :] = tutw.astype(tutw_ref.dtype)
                qk_pair.append(qk * decay)
            qk_ref[d, rs, :] = jnp.concatenate(qk_pair, axis=-1).astype(qk_ref.dtype)


def _pre(qh, kh, gates_c, gates_t):
    t = qh.shape[1]
    ct = gates_t.shape[2]
    return pl.pallas_call(
        functools.partial(_pre_kernel, nck=ct // CHUNK),
        grid=(t // ct, A_QK_HEADS),
        in_specs=[
            pl.BlockSpec((None, ct, HEAD), lambda i, h: (h, i, 0)),
            pl.BlockSpec((None, ct, HEAD), lambda i, h: (h, i, 0)),
            pl.BlockSpec((ct, 3 * LANES), lambda i, h: (i, 0)),
            pl.BlockSpec((None, 3 * LANES, ct), lambda i, h: (i, 0, 0)),
        ],
        out_specs=[
            pl.BlockSpec((2, 2, ct, HEAD), lambda i, h: (0, h, i, 0)),
            pl.BlockSpec((2, None, ct, HEAD), lambda i, h: (0, h, i, 0)),
        ],
        out_shape=[
            jax.ShapeDtypeStruct((2, A_V_HEADS, t, HEAD), BF16),
            jax.ShapeDtypeStruct((2, A_QK_HEADS, t, HEAD), BF16),
        ],
        compiler_params=_cparams(("parallel", "arbitrary")),
        name="chunk_pre",
    )(qh, kh, gates_c, gates_t)


def _rec_kernel(q_ref, k_ref, v_ref, tutw_ref, qk_ref, g_ref, s0_ref, o_ref, sfin_ref, s_ref, *, nc):
    d = pl.program_id(1)
    cidx = pl.program_id(2)

    @pl.when(cidx == 0)
    def _():
        s_ref[...] = s0_ref[...]

    def dir_lanes(x):
        return jnp.where(d == 0, x, pltpu.roll(x, LANES - 2, 1))

    g1 = dir_lanes(g_ref[:, LANES:2 * LANES])
    g2 = dir_lanes(g_ref[:, 2 * LANES:3 * LANES])
    for j in range(A_V_HEADS):
        hq, jj = divmod(j, 2)
        q = q_ref[hq]
        k = k_ref[hq]
        s = s_ref[j]
        r = _dot(jnp.concatenate([q, k], axis=0).astype(BF16), s.astype(BF16))
        qs = r[:CHUNK]
        ks = r[CHUNK:]
        rhs = jnp.concatenate([v_ref[j], -ks], axis=0).astype(BF16)
        v_new = _dot(tutw_ref[j], rhs)
        gl = 8 * hq + jj
        e_col = g1[:, gl:gl + 1]
        dec_col = g2[:, gl:gl + 1]
        eg = g2[0:1, 4 + gl:5 + gl]
        qkm = qk_ref[hq][:, jj * CHUNK:(jj + 1) * CHUNK]
        o_ref[j] = (e_col * qs + _dot(qkm, v_new.astype(BF16))).astype(o_ref.dtype)
        s_ref[j] = s * eg + _dot_tn(k.astype(BF16), (v_new * dec_col).astype(BF16))

    @pl.when(cidx == nc - 1)
    def _():
        sfin_ref[...] = s_ref[...]


def _rec(qh, kh, vh, tutw, qkm, gates_c, s0, n_seq, n_dir):
    t = qh.shape[1]
    nc = t // n_seq // CHUNK

    def chunk(b, d, c):
        return b * nc + c + d * (nc - 1 - 2 * c)

    return pl.pallas_call(
        functools.partial(_rec_kernel, nc=nc),
        grid=(n_seq, n_dir, nc),
        in_specs=[
            pl.BlockSpec((A_QK_HEADS, CHUNK, HEAD), lambda b, d, c: (0, chunk(b, d, c), 0)),
            pl.BlockSpec((A_QK_HEADS, CHUNK, HEAD), lambda b, d, c: (0, chunk(b, d, c), 0)),
            pl.BlockSpec((A_V_HEADS, CHUNK, HEAD), lambda b, d, c: (0, chunk(b, d, c), 0)),
            pl.BlockSpec((None, A_V_HEADS, CHUNK, HEAD), lambda b, d, c: (d, 0, chunk(b, d, c), 0)),
            pl.BlockSpec((None, A_QK_HEADS, CHUNK, HEAD), lambda b, d, c: (d, 0, chunk(b, d, c), 0)),
            pl.BlockSpec((CHUNK, 3 * LANES), lambda b, d, c: (chunk(b, d, c), 0)),
            pl.BlockSpec((None, None, A_V_HEADS, HEAD, HEAD), lambda b, d, c: (b, d, 0, 0, 0)),
        ],
        out_specs=[
            pl.BlockSpec((None, A_V_HEADS, CHUNK, HEAD), lambda b, d, c: (d, 0, chunk(b, d, c), 0)),
            pl.BlockSpec((None, None, A_V_HEADS, HEAD, HEAD), lambda b, d, c: (b, d, 0, 0, 0)),
        ],
        out_shape=[
            jax.ShapeDtypeStruct((n_dir, A_V_HEADS, t, HEAD), F32),
            jax.ShapeDtypeStruct((n_seq, n_dir, A_V_HEADS, HEAD, HEAD), F32),
        ],
        scratch_shapes=[pltpu.VMEM((A_V_HEADS, HEAD, HEAD), F32)],
        compiler_params=_cparams(("parallel", "parallel", "arbitrary")),
        name="chunk_rec",
    )(qh, kh, vh, tutw, qkm, gates_c, s0)


def _post_kernel(o_ref, z_ref, attn_ref, ga_ref, gb_ref, x_ref, na_ref, wa_ref, wb_ref, wo_ref,
                 nf_ref, wr_ref, br_ref, h_ref, xn_ref, comb_ref):
    parts = []
    for j in range(A_V_HEADS):
        o = o_ref[0, j] + o_ref[1, j]
        on = o * lax.rsqrt(jnp.mean(o * o, axis=-1, keepdims=True) + EPS) * na_ref[...]
        z = z_ref[:, j * HEAD:(j + 1) * HEAD].astype(F32)
        parts.append((on * (z * (1.0 / (1.0 + jnp.exp(-z))))).astype(BF16))
    ya = _dot(jnp.concatenate(parts, axis=-1), wa_ref[...])
    yb = _dot(attn_ref[...], wb_ref[...])
    sig_a = 1.0 / (1.0 + jnp.exp(-ga_ref[...].astype(F32)))
    sig_b = 1.0 / (1.0 + jnp.exp(-gb_ref[...].astype(F32)))
    merged = sig_a * ya + sig_b * yb
    h = x_ref[...] + _dot(merged.astype(BF16), wo_ref[...])
    h_ref[...] = h
    xn = h * lax.rsqrt(jnp.mean(h * h, axis=-1, keepdims=True) + EPS) * nf_ref[...]
    xn_ref[...] = xn.astype(xn_ref.dtype)

    logits = _dot_f32(xn, wr_ref[...]) + br_ref[...]
    lane = lax.broadcasted_iota(jnp.int32, logits.shape, 1)
    lane_f = lane.astype(F32)
    g_mask = (lane >= N_EXPERTS) & (lane < N_EXPERTS + N_GROUPS)
    gl = jnp.where(g_mask, logits, NEG_BIG)
    g_max = jnp.max(gl, axis=-1, keepdims=True)
    g_w = 1.0 / jnp.sum(jnp.exp(gl - g_max), axis=-1, keepdims=True)
    g_idx = jnp.min(jnp.where(gl == g_max, lane_f, 1e9), axis=-1, keepdims=True) - N_EXPERTS
    e_mask = (lane < N_EXPERTS) & ((lane // EXPERTS_PER_GROUP).astype(F32) == g_idx)
    el = jnp.where(e_mask, logits, NEG_BIG)
    m1 = jnp.max(el, axis=-1, keepdims=True)
    i1 = jnp.min(jnp.where(el == m1, lane_f, 1e9), axis=-1, keepdims=True)
    el2 = jnp.where(lane_f == i1, NEG_BIG, el)
    m2 = jnp.max(el2, axis=-1, keepdims=True)
    i2 = jnp.min(jnp.where(el2 == m2, lane_f, 1e9), axis=-1, keepdims=True)
    e2 = jnp.exp(m2 - m1)
    w1 = g_w / (1.0 + e2)
    w2 = g_w * e2 / (1.0 + e2)
    comb_ref[...] = jnp.where(lane_f == i1, w1, 0.0) + jnp.where(lane_f == i2, w2, 0.0)


def _post(o, proj, attn, x2, na, wa, wb, wo, nf, wr, br):
    t = x2.shape[0]
    tm = _tile(t, 256)
    const = lambda i: (0, 0)
    return pl.pallas_call(
        _post_kernel,
        grid=(t // tm,),
        in_specs=[
            pl.BlockSpec((2, A_V_HEADS, tm, HEAD), lambda i: (0, 0, i, 0)),
            pl.BlockSpec((tm, A_VAL_DIM), lambda i: (i, OFF_Z // A_VAL_DIM)),
            pl.BlockSpec((tm, B_Q_DIM), lambda i: (i, 0)),
            pl.BlockSpec((tm, D_MODEL), lambda i: (i, OFF_GA // D_MODEL)),
            pl.BlockSpec((tm, D_MODEL), lambda i: (i, OFF_GB // D_MODEL)),
            pl.BlockSpec((tm, D_MODEL), lambda i: (i, 0)),
            pl.BlockSpec((1, HEAD), const),
            pl.BlockSpec((A_VAL_DIM, D_MODEL), const),
            pl.BlockSpec((B_Q_DIM, D_MODEL), const),
            pl.BlockSpec((D_MODEL, D_MODEL), const),
            pl.BlockSpec((1, D_MODEL), const),
            pl.BlockSpec((D_MODEL, LANES), const),
            pl.BlockSpec((1, LANES), const),
        ],
        out_specs=[
            pl.BlockSpec((tm, D_MODEL), lambda i: (i, 0)),
            pl.BlockSpec((tm, D_MODEL), lambda i: (i, 0)),
            pl.BlockSpec((tm, LANES), lambda i: (i, 0)),
        ],
        out_shape=[
            jax.ShapeDtypeStruct((t, D_MODEL), F32),
            jax.ShapeDtypeStruct((t, D_MODEL), BF16),
            jax.ShapeDtypeStruct((t, LANES), F32),
        ],
        compiler_params=_cparams(("parallel",)),
        name="post_mixer",
    )(o, proj, attn, proj, proj, x2, na, wa, wb, wo, nf, wr, br)


def _moe_kernel(xn_ref, comb_ref, h_ref, wgu_ref, wd_ref, nfin_ref, y_ref, acc_ref):
    e = pl.program_id(1)

    @pl.when(e == 0)
    def _():
        acc_ref[...] = h_ref[...]

    gu = _dot(xn_ref[...], wgu_ref[...])
    gate = gu[:, :D_FF_EXPERT]
    up = gu[:, D_FF_EXPERT:]
    lane = lax.broadcasted_iota(jnp.int32, comb_ref.shape, 1)
    w = jnp.sum(jnp.where(lane == e, comb_ref[...], 0.0), axis=-1, keepdims=True)
    hid = gate * (1.0 / (1.0 + jnp.exp(-gate))) * up
    acc_ref[...] += w * _dot(hid.astype(BF16), wd_ref[...])

    @pl.when(e == N_EXPERTS - 1)
    def _():
        a = acc_ref[...]
        y_ref[...] = a * lax.rsqrt(jnp.mean(a * a, axis=-1, keepdims=True) + EPS) * nfin_ref[...]


def _moe(xn, comb, h, wgu, wd, nfin):
    t = xn.shape[0]
    tm = _tile(t, 1024)
    return pl.pallas_call(
        _moe_kernel,
        grid=(t // tm, N_EXPERTS),
        in_specs=[
            pl.BlockSpec((tm, D_MODEL), lambda i, e: (i, 0)),
            pl.BlockSpec((tm, LANES), lambda i, e: (i, 0)),
            pl.BlockSpec((tm, D_MODEL), lambda i, e: (i, 0)),
            pl.BlockSpec((None, D_MODEL, 2 * D_FF_EXPERT), lambda i, e: (e, 0, 0)),
            pl.BlockSpec((None, D_FF_EXPERT, D_MODEL), lambda i, e: (e, 0, 0)),
            pl.BlockSpec((1, D_MODEL), lambda i, e: (0, 0)),
        ],
        out_specs=pl.BlockSpec((tm, D_MODEL), lambda i, e: (i, 0)),
        out_shape=jax.ShapeDtypeStruct((t, D_MODEL), F32),
        scratch_shapes=[pltpu.VMEM((tm, D_MODEL), F32)],
        compiler_params=_cparams(("parallel", "arbitrary")),
        name="moe_experts",
    )(xn, comb, h, wgu, wd, nfin)


def _forward(x, p):
    n_seq, seq_len, _ = x.shape
    t = n_seq * seq_len
    x2 = x.reshape(t, D_MODEL)
    proj, ab = _inproj(x2, p["norm_mix"], p["w_main"], p["w_ab"])

    q_r, k_r = _attn_prep(proj, p["cos"], p["sin"], p["norm_q_b"], p["norm_k_b"], seq_len)
    attn = _flash(q_r, k_r, proj, p["k_meta"], p["v_meta"], p["meta_bias"], n_seq, seq_len)

    zeros_halo = jnp.zeros((n_seq, HALO, CONV_CH), BF16)
    first_rows = proj.reshape(n_seq, seq_len, N_MAIN)[:, :HALO, :CONV_CH]
    meta_x = jnp.broadcast_to(p["meta_chunk_x"][None], (n_seq, CHUNK, CONV_CH)).reshape(n_seq * CHUNK, CONV_CH)
    meta_ab = jnp.broadcast_to(p["meta_chunk_ab"][None], (n_seq, CHUNK, LANES)).reshape(n_seq * CHUNK, LANES)
    conv_args = (p["conv_w"], p["a_log"], p["dt_bias"])
    mq, mk, mv, mgc, mgt = _conv(meta_x, 0, zeros_halo, first_rows, meta_ab, *conv_args, tm=CHUNK, pad_rows=PAD_A)
    m_tutw, m_qk = _pre(mq, mk, mgc, mgt)
    zero_state = jnp.zeros((n_seq, 1, A_V_HEADS, HEAD, HEAD), F32)
    _, s_meta = _rec(mq, mk, mv, m_tutw, m_qk, mgc, zero_state, n_seq, 1)

    tm = _tile(seq_len, 256)
    per_seq = seq_len // tm
    tiles = proj.reshape(n_seq, per_seq, tm, N_MAIN)
    tails = tiles[:, :, tm - HALO:, :CONV_CH]
    heads = tiles[:, :, :HALO, :CONV_CH]
    meta_tail = jnp.broadcast_to(p["meta_chunk_x"][None, None, CHUNK - HALO:], (n_seq, 1, HALO, CONV_CH))
    prev = jnp.concatenate([meta_tail, tails[:, :-1]], axis=1).reshape(n_seq * per_seq, HALO, CONV_CH)
    nxt = jnp.concatenate([heads[:, 1:], zeros_halo[:, None]], axis=1).reshape(n_seq * per_seq, HALO, CONV_CH)
    qh, kh, vh, gates_c, gates_t = _conv(proj, OFF_QKV // CONV_CH, prev, nxt, ab, *conv_args, tm=tm, pad_rows=0)
    tutw, qkm = _pre(qh, kh, gates_c, gates_t)
    s0 = jnp.concatenate([s_meta, zero_state], axis=1)
    o, _ = _rec(qh, kh, vh, tutw, qkm, gates_c, s0, n_seq, 2)

    h, xn, comb = _post(o, proj, attn, x2, p["norm_out_a"], p["w_branch_a"], p["w_branch_b"], p["w_out"],
                        p["norm_ffn"], p["w_route"], p["b_route"])
    y = _moe(xn, comb, h, p["w_gate_up"], p["w_down"], p["norm_final"])
    return y.reshape(n_seq, seq_len, D_MODEL)


def _prepare(meta_tokens, norm_mix, w_in, conv_w, a_log_fwd, a_log_bwd, dt_bias_fwd, dt_bias_bwd,
             norm_out_a, norm_q_b, norm_k_b, w_branch_a, w_branch_b, w_out, norm_ffn, w_group, b_group,
             w_router, b_router, w_gate_e, w_up_e, w_down_e, norm_final, max_len):
    w = w_in[0]
    bounds = np.cumsum([0, A_KEY_DIM, A_KEY_DIM, A_VAL_DIM, A_VAL_DIM, 4 * A_V_HEADS, B_Q_DIM, B_KV_DIM,
                        B_KV_DIM, D_MODEL, D_MODEL])
    qa, ka, va, za, abw, qb, kb, vb, ga, gb = [w[:, bounds[i]:bounds[i + 1]] for i in range(10)]
    p = {}
    p["w_main"] = jnp.concatenate([qa, ka, va, za, qb, ga, gb, kb, vb], axis=1).astype(BF16)
    src = _gate_source_columns()
    p["w_ab"] = jnp.pad(abw[:, src], ((0, 0), (0, LANES - GATE_LANES))).astype(BF16)
    p["norm_mix"] = norm_mix[0][None]
    p["conv_w"] = jnp.pad(conv_w[0], ((0, HALO - CONV_W), (0, 0)))
    pad_lanes = lambda v: jnp.pad(v, (0, LANES - v.shape[0]))[None]
    zeros_b = jnp.zeros((2 * A_V_HEADS,), F32)
    p["a_log"] = pad_lanes(jnp.concatenate([a_log_fwd[0], a_log_bwd[0], zeros_b])[src])
    p["dt_bias"] = pad_lanes(jnp.concatenate([dt_bias_fwd[0], dt_bias_bwd[0], zeros_b])[src])
    p["norm_out_a"] = norm_out_a[0][None]
    p["norm_q_b"] = norm_q_b[0][None]
    p["norm_k_b"] = norm_k_b[0][None]
    p["w_branch_a"] = w_branch_a[0].astype(BF16)
    p["w_branch_b"] = w_branch_b[0].astype(BF16)
    p["w_out"] = w_out[0].astype(BF16)
    p["norm_ffn"] = norm_ffn[0][None]
    p["w_route"] = jnp.pad(jnp.concatenate([w_router[0], w_group[0]], axis=1),
                           ((0, 0), (0, LANES - N_EXPERTS - N_GROUPS)))
    p["b_route"] = pad_lanes(jnp.concatenate([b_router[0], b_group[0]]))
    p["w_gate_up"] = jnp.concatenate([w_gate_e[0], w_up_e[0]], axis=-1).astype(BF16)
    p["w_down"] = w_down_e[0].astype(BF16)
    p["norm_final"] = norm_final[None]

    pos = jnp.arange(max_len, dtype=jnp.int32)
    p["cos"], p["sin"] = _rope_tables((pos // GRID_W).astype(F32), (pos % GRID_W).astype(F32))
    cos_m, sin_m = _rope_tables(jnp.full((N_META,), -1.0, F32), jnp.arange(N_META, dtype=F32))

    mproj, mab = _inproj(meta_tokens, p["norm_mix"], p["w_main"], p["w_ab"])
    _, k_meta = _attn_prep(mproj, cos_m, sin_m, p["norm_q_b"], p["norm_k_b"], N_META)
    p["k_meta"] = jnp.pad(k_meta, ((0, LANES - N_META), (0, 0)))
    p["v_meta"] = jnp.pad(mproj[:, OFF_VB:OFF_VB + B_KV_DIM], ((0, LANES - N_META), (0, 0)))
    p["meta_bias"] = jnp.where(jnp.arange(LANES) < N_META, 0.0, NEG_BIG).astype(F32)[None]
    p["meta_chunk_x"] = jnp.pad(mproj[:, :CONV_CH], ((PAD_A, 0), (0, 0)))
    p["meta_chunk_ab"] = jnp.pad(mab, ((PAD_A, 0), (0, 0)))
    return p


def kernel(x_prompt, x_sample, meta_tokens, norm_mix, w_in, conv_w, a_log_fwd, a_log_bwd, dt_bias_fwd,
           dt_bias_bwd, norm_out_a, norm_q_b, norm_k_b, w_branch_a, w_branch_b, w_out, norm_ffn, w_group,
           b_group, w_router, b_router, w_gate_e, w_up_e, w_down_e, norm_final):
    max_len = max(x_prompt.shape[1], x_sample.shape[1])
    p = _prepare(meta_tokens, norm_mix, w_in, conv_w, a_log_fwd, a_log_bwd, dt_bias_fwd, dt_bias_bwd,
                 norm_out_a, norm_q_b, norm_k_b, w_branch_a, w_branch_b, w_out, norm_ffn, w_group, b_group,
                 w_router, b_router, w_gate_e, w_up_e, w_down_e, norm_final, max_len)
    return (_forward(x_prompt, p), _forward(x_sample, p))
```

```python
import functools
import math

import numpy as np
import jax
import jax.numpy as jnp
from jax import lax
from jax.experimental import pallas as pl
from jax.experimental.pallas import tpu as pltpu

F32 = jnp.float32
BF16 = jnp.bfloat16

D_MODEL = 1024
N_META = 16
GRID_W = 64
EPS = 1e-6
HEAD = 128
A_QK_HEADS = 8
A_V_HEADS = 16
A_KEY_DIM = A_QK_HEADS * HEAD
A_VAL_DIM = A_V_HEADS * HEAD
CONV_W = 5
CONV_CH = 2 * A_KEY_DIM + A_VAL_DIM
CHUNK = 64
PAD_A = CHUNK - N_META
B_Q_HEADS = 8
B_KV_HEADS = 2
B_GROUPS = B_Q_HEADS // B_KV_HEADS
B_Q_DIM = B_Q_HEADS * HEAD
B_KV_DIM = B_KV_HEADS * HEAD
ROPE_THETA = 10000.0
N_GROUPS = 4
EXPERTS_PER_GROUP = 8
N_EXPERTS = N_GROUPS * EXPERTS_PER_GROUP
D_FF_EXPERT = 256

OFF_QKV = 0
OFF_Z = 4096
OFF_QB = 6144
OFF_GA = 7168
OFF_GB = 8192
OFF_KB = 9216
OFF_VB = 9472
N_MAIN = 9728
LANES = 128
GATE_LANES = 4 * A_V_HEADS
NEG_BIG = -1e30


def _gate_source_columns():
    src = []
    for lane in range(GATE_LANES):
        hq, kind, d, jj = lane // 8, (lane % 8) // 4, (lane % 4) // 2, lane % 2
        src.append(2 * A_V_HEADS * kind + A_V_HEADS * d + 2 * hq + jj)
    return np.asarray(src)
VMEM_LIMIT = 56 * 1024 * 1024


def _cparams(sem):
    return pltpu.CompilerParams(dimension_semantics=sem, vmem_limit_bytes=VMEM_LIMIT)


def _dot(a, b):
    return jnp.dot(a, b, preferred_element_type=F32)


def _dot_nt(a, b):
    return lax.dot_general(a, b, (((1,), (1,)), ((), ())), preferred_element_type=F32)


def _dot_tn(a, b):
    return lax.dot_general(a, b, (((0,), (0,)), ((), ())), preferred_element_type=F32)


def _dot_f32(a, b):
    return jnp.dot(a, b, preferred_element_type=F32, precision=lax.Precision.HIGHEST)


def _tile(n, pref):
    t = min(n, pref)
    assert n % t == 0, (n, t)
    return t


def _inproj_kernel(x_ref, g_ref, w_ref, wab_ref, o_ref, oab_ref, xn_ref):
    @pl.when(pl.program_id(1) == 0)
    def _():
        x = x_ref[...]
        xn = x * lax.rsqrt(jnp.mean(x * x, axis=-1, keepdims=True) + EPS) * g_ref[...]
        xb = xn.astype(BF16)
        xn_ref[...] = xb
        oab_ref[...] = _dot(xb, wab_ref[...])

    o_ref[...] = _dot(xn_ref[...], w_ref[...]).astype(o_ref.dtype)


def _inproj(x2, gain, w_main, w_ab):
    t = x2.shape[0]
    tm = _tile(t, 1024)
    tn = 512
    return pl.pallas_call(
        _inproj_kernel,
        grid=(t // tm, N_MAIN // tn),
        in_specs=[
            pl.BlockSpec((tm, D_MODEL), lambda i, j: (i, 0)),
            pl.BlockSpec((1, D_MODEL), lambda i, j: (0, 0)),
            pl.BlockSpec((D_MODEL, tn), lambda i, j: (0, j)),
            pl.BlockSpec((D_MODEL, LANES), lambda i, j: (0, 0)),
        ],
        out_specs=[
            pl.BlockSpec((tm, tn), lambda i, j: (i, j)),
            pl.BlockSpec((tm, LANES), lambda i, j: (i, 0)),
        ],
        out_shape=[jax.ShapeDtypeStruct((t, N_MAIN), BF16), jax.ShapeDtypeStruct((t, LANES), F32)],
        scratch_shapes=[pltpu.VMEM((tm, D_MODEL), BF16)],
        compiler_params=_cparams(("parallel", "arbitrary")),
        name="in_proj",
    )(x2, gain, w_main, w_ab)


def _rope_tables(row, col):
    half = HEAD // 2
    inv_freq = ROPE_THETA ** (-jnp.arange(0, half, 2, dtype=F32) / half)
    ang_r = row[:, None] * inv_freq
    ang_c = col[:, None] * inv_freq
    cos = jnp.concatenate([jnp.cos(ang_r)] * 2 + [jnp.cos(ang_c)] * 2, axis=-1)
    sin = jnp.concatenate([-jnp.sin(ang_r), jnp.sin(ang_r), -jnp.sin(ang_c), jnp.sin(ang_c)], axis=-1)
    return cos, sin


def _attn_prep_kernel(q_ref, k_ref, cos_ref, sin_ref, gq_ref, gk_ref, qo_ref, ko_ref):
    cos = cos_ref[...]
    sin = sin_ref[...]
    lane = lax.broadcasted_iota(jnp.int32, cos.shape, 1)
    low = (lane % (HEAD // 2)) < (HEAD // 4)

    def prep(x, gain, scale):
        x = x.astype(F32)
        xn = x * lax.rsqrt(jnp.mean(x * x, axis=-1, keepdims=True) + EPS) * gain
        partner = jnp.where(low, pltpu.roll(xn, HEAD - HEAD // 4, 1), pltpu.roll(xn, HEAD // 4, 1))
        return (xn * cos + partner * sin) * scale

    for h in range(B_Q_HEADS):
        sl = slice(h * HEAD, (h + 1) * HEAD)
        qo_ref[:, sl] = prep(q_ref[:, sl], gq_ref[...], HEAD ** -0.5 * math.log2(math.e)).astype(qo_ref.dtype)
    for h in range(B_KV_HEADS):
        sl = slice(h * HEAD, (h + 1) * HEAD)
        ko_ref[:, sl] = prep(k_ref[:, sl], gk_ref[...], 1.0).astype(ko_ref.dtype)


def _attn_prep(proj, cos, sin, gq, gk, seq_len):
    t = proj.shape[0]
    tm = _tile(seq_len, 512)
    per_seq = seq_len // tm
    return pl.pallas_call(
        _attn_prep_kernel,
        grid=(t // tm,),
        in_specs=[
            pl.BlockSpec((tm, B_Q_DIM), lambda i: (i, OFF_QB // B_Q_DIM)),
            pl.BlockSpec((tm, B_KV_DIM), lambda i: (i, OFF_KB // B_KV_DIM)),
            pl.BlockSpec((tm, HEAD), lambda i: (i % per_seq, 0)),
            pl.BlockSpec((tm, HEAD), lambda i: (i % per_seq, 0)),
            pl.BlockSpec((1, HEAD), lambda i: (0, 0)),
            pl.BlockSpec((1, HEAD), lambda i: (0, 0)),
        ],
        out_specs=[
            pl.BlockSpec((tm, B_Q_DIM), lambda i: (i, 0)),
            pl.BlockSpec((tm, B_KV_DIM), lambda i: (i, 0)),
        ],
        out_shape=[jax.ShapeDtypeStruct((t, B_Q_DIM), BF16), jax.ShapeDtypeStruct((t, B_KV_DIM), BF16)],
        compiler_params=_cparams(("parallel",)),
        name="attn_prep",
    )(proj, proj, cos, sin, gq, gk)


FLASH_RB = 64


def _flash_kernel(q_ref, k_ref, v_ref, km_ref, vm_ref, bias_ref, o_ref,
                  qa_ref, sa_ref, sb_ref, p_ref, m_ref, l_ref, acc_ref, *, tq, tk, nkb):
    rows = B_GROUPS * tq
    ncol = tk // LANES
    for g in range(B_GROUPS):
        qa_ref[g * tq:(g + 1) * tq, :] = q_ref[:, g * HEAD:(g + 1) * HEAD]

    s = _dot_nt(qa_ref[...], km_ref[...]) + bias_ref[...]
    m = jnp.max(s, axis=-1, keepdims=True)
    p = jnp.exp2(s - m)
    m_ref[...] = jnp.broadcast_to(m, (rows, LANES))
    l_ref[...] = p
    acc_ref[...] = _dot(p.astype(BF16), vm_ref[...])

    def scores(s_ref, kb):
        start = pl.multiple_of(kb * tk, tk)
        s_ref[...] = _dot_nt(qa_ref[...], k_ref[pl.ds(start, tk), :])

    def softmax_block(s_ref, kb):
        for rb in range(rows // FLASH_RB):
            rs = slice(rb * FLASH_RB, (rb + 1) * FLASH_RB)
            cols = [s_ref[rs, c * LANES:(c + 1) * LANES] for c in range(ncol)]
            mx = cols[0]
            for col in cols[1:]:
                mx = jnp.maximum(mx, col)
            m_old = m_ref[rs, :]
            m_new = jnp.maximum(m_old, jnp.max(mx, axis=-1, keepdims=True))
            alpha = jnp.exp2(m_old - m_new)
            psum = None
            for c, col in enumerate(cols):
                pc = jnp.exp2(col - m_new)
                psum = pc if psum is None else psum + pc
                p_ref[rs, c * LANES:(c + 1) * LANES] = pc.astype(BF16)
            m_ref[rs, :] = m_new
            l_ref[rs, :] = alpha * l_ref[rs, :] + psum
            acc_ref[rs, :] = alpha * acc_ref[rs, :]
        start = pl.multiple_of(kb * tk, tk)
        acc_ref[...] += _dot(p_ref[...], v_ref[pl.ds(start, tk), :])

    scores(sa_ref, 0)
    if nkb > 1:
        def body(i, carry):
            scores(sb_ref, 2 * i + 1)
            softmax_block(sa_ref, 2 * i)
            scores(sa_ref, 2 * i + 2)
            softmax_block(sb_ref, 2 * i + 1)
            return carry

        lax.fori_loop(0, nkb // 2 - 1, body, 0)
        scores(sb_ref, nkb - 1)
        softmax_block(sa_ref, nkb - 2)
        softmax_block(sb_ref, nkb - 1)
    else:
        softmax_block(sa_ref, 0)
    inv_l = 1.0 / jnp.sum(l_ref[...], axis=-1, keepdims=True)
    for g in range(B_GROUPS):
        gs = slice(g * tq, (g + 1) * tq)
        o_ref[:, g * HEAD:(g + 1) * HEAD] = (acc_ref[gs, :] * inv_l[gs]).astype(o_ref.dtype)


def _flash(q_r, k_r, proj, km, vm, bias, n_seq, seq_len):
    tq = _tile(seq_len, 256)
    tk = _tile(seq_len, 512)
    assert seq_len // tk == 1 or (seq_len // tk) % 2 == 0
    rows = B_GROUPS * tq
    q3 = q_r.reshape(n_seq, seq_len, B_Q_DIM)
    k3 = k_r.reshape(n_seq, seq_len, B_KV_DIM)
    p3 = proj.reshape(n_seq, seq_len, N_MAIN)
    gw = B_GROUPS * HEAD
    out = pl.pallas_call(
        functools.partial(_flash_kernel, tq=tq, tk=tk, nkb=seq_len // tk),
        grid=(n_seq, B_KV_HEADS, seq_len // tq),
        in_specs=[
            pl.BlockSpec((None, tq, gw), lambda b, j, i: (b, i, j)),
            pl.BlockSpec((None, seq_len, HEAD), lambda b, j, i: (b, 0, j)),
            pl.BlockSpec((None, seq_len, HEAD), lambda b, j, i: (b, 0, OFF_VB // HEAD + j)),
            pl.BlockSpec((LANES, HEAD), lambda b, j, i: (0, j)),
            pl.BlockSpec((LANES, HEAD), lambda b, j, i: (0, j)),
            pl.BlockSpec((1, LANES), lambda b, j, i: (0, 0)),
        ],
        out_specs=pl.BlockSpec((None, tq, gw), lambda b, j, i: (b, i, j)),
        out_shape=jax.ShapeDtypeStruct((n_seq, seq_len, B_Q_DIM), BF16),
        scratch_shapes=[
            pltpu.VMEM((rows, HEAD), BF16),
            pltpu.VMEM((rows, tk), F32),
            pltpu.VMEM((rows, tk), F32),
            pltpu.VMEM((rows, tk), BF16),
            pltpu.VMEM((rows, LANES), F32),
            pltpu.VMEM((rows, LANES), F32),
            pltpu.VMEM((rows, HEAD), F32),
        ],
        compiler_params=_cparams(("parallel", "parallel", "arbitrary")),
        name="flash_attn",
    )(q3, k3, p3, km, vm, bias)
    return out.reshape(n_seq * seq_len, B_Q_DIM)


CONV_CB = 512
HALO = 8


def _conv_kernel(x_ref, prev_ref, next_ref, ab_ref, cw_ref, alog_ref, dtb_ref,
                 q_ref, k_ref, v_ref, gc_ref, gt_ref, xs_ref, *, tm, pad_rows):
    row = lax.broadcasted_iota(jnp.int32, (tm, 1), 0)
    valid = (row % CHUNK) >= pad_rows
    for cb in range(CONV_CH // CONV_CB):
        cs = slice(cb * CONV_CB, (cb + 1) * CONV_CB)
        xs_ref[0:HALO, :] = prev_ref[0, :, cs].astype(F32)
        xs_ref[HALO:HALO + tm, :] = x_ref[:, cs].astype(F32)
        xs_ref[HALO + tm:, :] = next_ref[0, :, cs].astype(F32)
        y = jnp.zeros((tm, CONV_CB), F32)
        for w in range(CONV_W):
            y = y + xs_ref[pl.ds(HALO - CONV_W // 2 + w, tm), :] * cw_ref[w:w + 1, cs]
        y = y * (1.0 / (1.0 + jnp.exp(-y)))
        if pad_rows:
            y = jnp.where(valid, y, 0.0)
        for hh in range(CONV_CB // HEAD):
            col = cb * CONV_CB + hh * HEAD
            yh = y[:, hh * HEAD:(hh + 1) * HEAD]
            if col < 2 * A_KEY_DIM:
                yh = yh * lax.rsqrt(jnp.sum(yh * yh, axis=-1, keepdims=True) + EPS)
            if col < A_KEY_DIM:
                q_ref[col // HEAD] = (yh * (HEAD ** -0.5)).astype(q_ref.dtype)
            elif col < 2 * A_KEY_DIM:
                k_ref[(col - A_KEY_DIM) // HEAD] = yh.astype(k_ref.dtype)
            else:
                v_ref[(col - 2 * A_KEY_DIM) // HEAD] = yh.astype(v_ref.dtype)

    ab = ab_ref[...]
    lane = lax.broadcasted_iota(jnp.int32, (tm, LANES), 1)
    is_gate = lane < GATE_LANES
    is_a = is_gate & ((lane % 8) < 4)
    z = ab + dtb_ref[...]
    softplus = jnp.maximum(z, 0.0) + jnp.log1p(jnp.exp(-jnp.abs(z)))
    g = -jnp.exp(alog_ref[...]) * softplus
    beta = 1.0 / (1.0 + jnp.exp(-ab))
    g = jnp.where(valid & is_a, g, 0.0)
    beta = jnp.where(valid & is_gate & (~is_a), beta, 0.0)

    r = lax.broadcasted_iota(jnp.int32, (CHUNK, CHUNK), 0)
    c = lax.broadcasted_iota(jnp.int32, (CHUNK, CHUNK), 1)
    lower = (c <= r).astype(F32)
    upper = (c >= r).astype(F32)
    ones = jnp.ones((CHUNK, CHUNK), F32)
    lane_c = lax.broadcasted_iota(jnp.int32, (CHUNK, LANES), 1)
    fwd_lane = (lane_c % 4) < 2
    a_lane = (lane_c % 8) < 4
    for ck in range(tm // CHUNK):
        rs = slice(ck * CHUNK, (ck + 1) * CHUNK)
        gk = g[rs]
        gcum = jnp.where(fwd_lane, _dot_f32(lower, gk), _dot_f32(upper, gk))
        gtot = _dot_f32(ones, gk)
        e = jnp.exp(gcum)
        dec = jnp.exp(gtot - gcum)
        eg = jnp.exp(gtot)
        bk = beta[rs]
        t0 = jnp.where(a_lane, gcum, bk)
        t1 = jnp.where(a_lane, e, bk * pltpu.roll(e, 4, 1))
        t2 = jnp.where(a_lane, dec, pltpu.roll(eg, 4, 1))
        gc_ref[rs, 0:LANES] = t0
        gc_ref[rs, LANES:2 * LANES] = t1
        gc_ref[rs, 2 * LANES:3 * LANES] = t2
        gt_ref[0, 0:LANES, rs] = t0.T
        gt_ref[0, LANES:2 * LANES, rs] = t1.T
        gt_ref[0, 2 * LANES:3 * LANES, rs] = t2.T


def _conv(x, col_block, prev, nxt, ab, cw, alog, dtb, tm, pad_rows):
    t = x.shape[0]
    nt = t // tm
    return pl.pallas_call(
        functools.partial(_conv_kernel, tm=tm, pad_rows=pad_rows),
        grid=(nt,),
        in_specs=[
            pl.BlockSpec((tm, CONV_CH), lambda i: (i, col_block)),
            pl.BlockSpec((1, HALO, CONV_CH), lambda i: (i, 0, 0)),
            pl.BlockSpec((1, HALO, CONV_CH), lambda i: (i, 0, 0)),
            pl.BlockSpec((tm, LANES), lambda i: (i, 0)),
            pl.BlockSpec((HALO, CONV_CH), lambda i: (0, 0)),
            pl.BlockSpec((1, LANES), lambda i: (0, 0)),
            pl.BlockSpec((1, LANES), lambda i: (0, 0)),
        ],
        out_specs=[
            pl.BlockSpec((A_QK_HEADS, tm, HEAD), lambda i: (0, i, 0)),
            pl.BlockSpec((A_QK_HEADS, tm, HEAD), lambda i: (0, i, 0)),
            pl.BlockSpec((A_V_HEADS, tm, HEAD), lambda i: (0, i, 0)),
            pl.BlockSpec((tm, 3 * LANES), lambda i: (i, 0)),
            pl.BlockSpec((1, 3 * LANES, tm), lambda i: (i, 0, 0)),
        ],
        out_shape=[
            jax.ShapeDtypeStruct((A_QK_HEADS, t, HEAD), BF16),
            jax.ShapeDtypeStruct((A_QK_HEADS, t, HEAD), BF16),
            jax.ShapeDtypeStruct((A_V_HEADS, t, HEAD), BF16),
            jax.ShapeDtypeStruct((t, 3 * LANES), F32),
            jax.ShapeDtypeStruct((nt, 3 * LANES, tm), F32),
        ],
        scratch_shapes=[pltpu.VMEM((tm + 2 * HALO, CONV_CB), F32)],
        compiler_params=_cparams(("parallel",)),
        name="conv_gates",
    )(x, prev, nxt, ab, cw, alog, dtb)


def _pre_kernel(q_ref, k_ref, gc_ref, gt_ref, tutw_ref, qk_ref, *, nck):
    hq = pl.program_id(1)
    r = lax.broadcasted_iota(jnp.int32, (CHUNK, LANES), 0)
    lane = lax.broadcasted_iota(jnp.int32, (CHUNK, LANES), 1)
    c = lane & (CHUNK - 1)
    left = lane < CHUNK

    def same(k):
        return (r >> k) == (c >> k)

    eye = jnp.where(r == c, 1.0, 0.0)
    same1 = same(1)
    levels = []
    k = 1
    while (1 << k) < CHUNK:
        levels.append(same(k + 1) & (~same(k)))
        k += 1

    def blockdiag(m):
        mb = m.astype(BF16)
        zero = jnp.zeros_like(mb)
        return jnp.concatenate([jnp.where(left, mb, zero), jnp.where(left, zero, mb)], axis=0)

    def pair_cols(x, rs, i):
        return jnp.where(left, x[rs, i:i + 1], x[rs, i + 1:i + 2])

    def pair_rows(x, i):
        return jnp.concatenate([x[i:i + 1], x[i + 1:i + 2]], axis=-1)

    g0 = pltpu.roll(gc_ref[:, 0:LANES], lax.rem(LANES - 8 * hq, LANES), 1)
    row0 = pl.multiple_of(8 * hq, 8)
    chains = []
    for ck in range(nck):
        rs = slice(ck * CHUNK, (ck + 1) * CHUNK)
        qb = q_ref[rs, :]
        kb = k_ref[rs, :]
        k2 = jnp.concatenate([kb, kb], axis=0)
        kk = _dot_nt(kb, k2)
        qk = _dot_nt(qb, k2)
        rows0 = gt_ref[pl.ds(row0, 8), rs]
        rows1 = gt_ref[pl.ds(LANES + row0, 8), rs]
        for d in range(2):
            incl = (c <= r) if d == 0 else (c >= r)
            strict = (c < r) if d == 0 else (c > r)
            gl = 2 * d
            gdiff = pair_cols(g0, rs, gl) - pair_rows(rows0, gl)
            decay = jnp.where(incl, jnp.exp(jnp.where(incl, gdiff, 0.0)), 0.0)
            qk_ref[d, rs, :] = (qk * decay).astype(qk_ref.dtype)
            a = jnp.where(strict, kk * decay, 0.0) * pair_cols(g0, rs, 4 + gl)
            chains.append(dict(a=a, t=eye - jnp.where(same1, a, 0.0), d=d, rs=rs,
                               beta=pair_rows(rows0, 4 + gl), be=pair_rows(rows1, 4 + gl)))
    for mask in levels:
        xs = [_dot(jnp.where(mask, ch["a"], 0.0).astype(BF16), blockdiag(ch["t"])) for ch in chains]
        ys = [_dot(ch["t"].astype(BF16), blockdiag(x)) for ch, x in zip(chains, xs)]
        for ch, y in zip(chains, ys):
            ch["t"] = ch["t"] - y
    for ch in chains:
        tu = ch["t"] * ch["beta"]
        tw = ch["t"] * ch["be"]
        d, rs = ch["d"], ch["rs"]
        tutw_ref[d, 0, rs, :] = jnp.where(left, tu, pltpu.roll(tw, CHUNK, 1)).astype(tutw_ref.dtype)
        tutw_ref[d, 1, rs, :] = jnp.where(left, pltpu.roll(tu, CHUNK, 1), tw).astype(tutw_ref.dtype)


def _pre(qh, kh, gates_c, gates_t):
    t = qh.shape[1]
    ct = gates_t.shape[2]
    return pl.pallas_call(
        functools.partial(_pre_kernel, nck=ct // CHUNK),
        grid=(t // ct, A_QK_HEADS),
        in_specs=[
            pl.BlockSpec((None, ct, HEAD), lambda i, h: (h, i, 0)),
            pl.BlockSpec((None, ct, HEAD), lambda i, h: (h, i, 0)),
            pl.BlockSpec((ct, 3 * LANES), lambda i, h: (i, 0)),
            pl.BlockSpec((None, 3 * LANES, ct), lambda i, h: (i, 0, 0)),
        ],
        out_specs=[
            pl.BlockSpec((2, 2, ct, HEAD), lambda i, h: (0, h, i, 0)),
            pl.BlockSpec((2, None, ct, HEAD), lambda i, h: (0, h, i, 0)),
        ],
        out_shape=[
            jax.ShapeDtypeStruct((2, A_V_HEADS, t, HEAD), BF16),
            jax.ShapeDtypeStruct((2, A_QK_HEADS, t, HEAD), BF16),
        ],
        compiler_params=_cparams(("parallel", "arbitrary")),
        name="chunk_pre",
    )(qh, kh, gates_c, gates_t)


def _rec_kernel(q_ref, k_ref, v_ref, tutw_ref, qk_ref, g_ref, s0_ref, o_ref, sfin_ref, s_ref, *, nc):
    d = pl.program_id(1)
    cidx = pl.program_id(2)

    @pl.when(cidx == 0)
    def _():
        s_ref[...] = s0_ref[...]

    def dir_lanes(x):
        return jnp.where(d == 0, x, pltpu.roll(x, LANES - 2, 1))

    g1 = dir_lanes(g_ref[:, LANES:2 * LANES])
    g2 = dir_lanes(g_ref[:, 2 * LANES:3 * LANES])
    heads = range(A_V_HEADS)
    qk_lhs = [jnp.concatenate([q_ref[hq], k_ref[hq]], axis=0) for hq in range(A_QK_HEADS)]
    k_t = [k_ref[hq].astype(F32).T.astype(BF16) for hq in range(A_QK_HEADS)]
    states = [s_ref[j] for j in heads]
    qs_ks = [_dot(qk_lhs[j // 2], states[j].astype(BF16)) for j in heads]
    v_new = [_dot(tutw_ref[j], jnp.concatenate([v_ref[j], (-qs_ks[j][CHUNK:]).astype(BF16)], axis=0))
             for j in heads]
    for j in heads:
        hq, jj = divmod(j, 2)
        gl = 8 * hq + jj
        qkm = qk_ref[hq][:, jj * CHUNK:(jj + 1) * CHUNK]
        o = g1[:, gl:gl + 1] * qs_ks[j][:CHUNK] + _dot(qkm, v_new[j].astype(BF16))
        o_ref[j] = o.astype(o_ref.dtype)
    for j in heads:
        hq, jj = divmod(j, 2)
        gl = 8 * hq + jj
        v_dec = (v_new[j] * g2[:, gl:gl + 1]).astype(BF16)
        s_ref[j] = states[j] * g2[0:1, 4 + gl:5 + gl] + _dot(k_t[hq], v_dec)

    @pl.when(cidx == nc - 1)
    def _():
        sfin_ref[...] = s_ref[...]


def _rec(qh, kh, vh, tutw, qkm, gates_c, s0, n_seq, n_dir):
    t = qh.shape[1]
    nc = t // n_seq // CHUNK

    def chunk(b, d, c):
        return b * nc + c + d * (nc - 1 - 2 * c)

    return pl.pallas_call(
        functools.partial(_rec_kernel, nc=nc),
        grid=(n_seq, n_dir, nc),
        in_specs=[
            pl.BlockSpec((A_QK_HEADS, CHUNK, HEAD), lambda b, d, c: (0, chunk(b, d, c), 0)),
            pl.BlockSpec((A_QK_HEADS, CHUNK, HEAD), lambda b, d, c: (0, chunk(b, d, c), 0)),
            pl.BlockSpec((A_V_HEADS, CHUNK, HEAD), lambda b, d, c: (0, chunk(b, d, c), 0)),
            pl.BlockSpec((None, A_V_HEADS, CHUNK, HEAD), lambda b, d, c: (d, 0, chunk(b, d, c), 0)),
            pl.BlockSpec((None, A_QK_HEADS, CHUNK, HEAD), lambda b, d, c: (d, 0, chunk(b, d, c), 0)),
            pl.BlockSpec((CHUNK, 3 * LANES), lambda b, d, c: (chunk(b, d, c), 0)),
            pl.BlockSpec((None, None, A_V_HEADS, HEAD, HEAD), lambda b, d, c: (b, d, 0, 0, 0)),
        ],
        out_specs=[
            pl.BlockSpec((None, A_V_HEADS, CHUNK, HEAD), lambda b, d, c: (d, 0, chunk(b, d, c), 0)),
            pl.BlockSpec((None, None, A_V_HEADS, HEAD, HEAD), lambda b, d, c: (b, d, 0, 0, 0)),
        ],
        out_shape=[
            jax.ShapeDtypeStruct((n_dir, A_V_HEADS, t, HEAD), F32),
            jax.ShapeDtypeStruct((n_seq, n_dir, A_V_HEADS, HEAD, HEAD), F32),
        ],
        scratch_shapes=[pltpu.VMEM((A_V_HEADS, HEAD, HEAD), F32)],
        compiler_params=_cparams(("parallel", "parallel", "arbitrary")),
        name="chunk_rec",
    )(qh, kh, vh, tutw, qkm, gates_c, s0)


def _post_kernel(o_ref, z_ref, attn_ref, ga_ref, gb_ref, x_ref, na_ref, wa_ref, wb_ref, wo_ref,
                 nf_ref, wr_ref, br_ref, h_ref, xn_ref, comb_ref):
    parts = []
    for j in range(A_V_HEADS):
        o = o_ref[0, j] + o_ref[1, j]
        on = o * lax.rsqrt(jnp.mean(o * o, axis=-1, keepdims=True) + EPS) * na_ref[...]
        z = z_ref[:, j * HEAD:(j + 1) * HEAD].astype(F32)
        parts.append((on * (z * (1.0 / (1.0 + jnp.exp(-z))))).astype(BF16))
    ya = _dot(jnp.concatenate(parts, axis=-1), wa_ref[...])
    yb = _dot(attn_ref[...], wb_ref[...])
    sig_a = 1.0 / (1.0 + jnp.exp(-ga_ref[...].astype(F32)))
    sig_b = 1.0 / (1.0 + jnp.exp(-gb_ref[...].astype(F32)))
    merged = sig_a * ya + sig_b * yb
    h = x_ref[...] + _dot(merged.astype(BF16), wo_ref[...])
    h_ref[...] = h
    xn = h * lax.rsqrt(jnp.mean(h * h, axis=-1, keepdims=True) + EPS) * nf_ref[...]
    xn_ref[...] = xn.astype(xn_ref.dtype)

    logits = _dot_f32(xn, wr_ref[...]) + br_ref[...]
    lane = lax.broadcasted_iota(jnp.int32, logits.shape, 1)
    lane_f = lane.astype(F32)
    g_mask = (lane >= N_EXPERTS) & (lane < N_EXPERTS + N_GROUPS)
    gl = jnp.where(g_mask, logits, NEG_BIG)
    g_max = jnp.max(gl, axis=-1, keepdims=True)
    g_w = 1.0 / jnp.sum(jnp.exp(gl - g_max), axis=-1, keepdims=True)
    g_idx = jnp.min(jnp.where(gl == g_max, lane_f, 1e9), axis=-1, keepdims=True) - N_EXPERTS
    e_mask = (lane < N_EXPERTS) & ((lane // EXPERTS_PER_GROUP).astype(F32) == g_idx)
    el = jnp.where(e_mask, logits, NEG_BIG)
    m1 = jnp.max(el, axis=-1, keepdims=True)
    i1 = jnp.min(jnp.where(el == m1, lane_f, 1e9), axis=-1, keepdims=True)
    el2 = jnp.where(lane_f == i1, NEG_BIG, el)
    m2 = jnp.max(el2, axis=-1, keepdims=True)
    i2 = jnp.min(jnp.where(el2 == m2, lane_f, 1e9), axis=-1, keepdims=True)
    e2 = jnp.exp(m2 - m1)
    w1 = g_w / (1.0 + e2)
    w2 = g_w * e2 / (1.0 + e2)
    comb_ref[...] = jnp.where(lane_f == i1, w1, 0.0) + jnp.where(lane_f == i2, w2, 0.0)


def _post(o, proj, attn, x2, na, wa, wb, wo, nf, wr, br):
    t = x2.shape[0]
    tm = _tile(t, 256)
    const = lambda i: (0, 0)
    return pl.pallas_call(
        _post_kernel,
        grid=(t // tm,),
        in_specs=[
            pl.BlockSpec((2, A_V_HEADS, tm, HEAD), lambda i: (0, 0, i, 0)),
            pl.BlockSpec((tm, A_VAL_DIM), lambda i: (i, OFF_Z // A_VAL_DIM)),
            pl.BlockSpec((tm, B_Q_DIM), lambda i: (i, 0)),
            pl.BlockSpec((tm, D_MODEL), lambda i: (i, OFF_GA // D_MODEL)),
            pl.BlockSpec((tm, D_MODEL), lambda i: (i, OFF_GB // D_MODEL)),
            pl.BlockSpec((tm, D_MODEL), lambda i: (i, 0)),
            pl.BlockSpec((1, HEAD), const),
            pl.BlockSpec((A_VAL_DIM, D_MODEL), const),
            pl.BlockSpec((B_Q_DIM, D_MODEL), const),
            pl.BlockSpec((D_MODEL, D_MODEL), const),
            pl.BlockSpec((1, D_MODEL), const),
            pl.BlockSpec((D_MODEL, LANES), const),
            pl.BlockSpec((1, LANES), const),
        ],
        out_specs=[
            pl.BlockSpec((tm, D_MODEL), lambda i: (i, 0)),
            pl.BlockSpec((tm, D_MODEL), lambda i: (i, 0)),
            pl.BlockSpec((tm, LANES), lambda i: (i, 0)),
        ],
        out_shape=[
            jax.ShapeDtypeStruct((t, D_MODEL), F32),
            jax.ShapeDtypeStruct((t, D_MODEL), BF16),
            jax.ShapeDtypeStruct((t, LANES), F32),
        ],
        compiler_params=_cparams(("parallel",)),
        name="post_mixer",
    )(o, proj, attn, proj, proj, x2, na, wa, wb, wo, nf, wr, br)


def _moe_kernel(xn_ref, comb_ref, h_ref, wgu_ref, wd_ref, nfin_ref, y_ref, acc_ref):
    e = pl.program_id(1)

    @pl.when(e == 0)
    def _():
        acc_ref[...] = h_ref[...]

    gu = _dot(xn_ref[...], wgu_ref[...])
    gate = gu[:, :D_FF_EXPERT]
    up = gu[:, D_FF_EXPERT:]
    lane = lax.broadcasted_iota(jnp.int32, comb_ref.shape, 1)
    w = jnp.sum(jnp.where(lane == e, comb_ref[...], 0.0), axis=-1, keepdims=True)
    hid = gate * (1.0 / (1.0 + jnp.exp(-gate))) * up
    acc_ref[...] += w * _dot(hid.astype(BF16), wd_ref[...])

    @pl.when(e == N_EXPERTS - 1)
    def _():
        a = acc_ref[...]
        y_ref[...] = a * lax.rsqrt(jnp.mean(a * a, axis=-1, keepdims=True) + EPS) * nfin_ref[...]


def _moe(xn, comb, h, wgu, wd, nfin):
    t = xn.shape[0]
    tm = _tile(t, 1024)
    return pl.pallas_call(
        _moe_kernel,
        grid=(t // tm, N_EXPERTS),
        in_specs=[
            pl.BlockSpec((tm, D_MODEL), lambda i, e: (i, 0)),
            pl.BlockSpec((tm, LANES), lambda i, e: (i, 0)),
            pl.BlockSpec((tm, D_MODEL), lambda i, e: (i, 0)),
            pl.BlockSpec((None, D_MODEL, 2 * D_FF_EXPERT), lambda i, e: (e, 0, 0)),
            pl.BlockSpec((None, D_FF_EXPERT, D_MODEL), lambda i, e: (e, 0, 0)),
            pl.BlockSpec((1, D_MODEL), lambda i, e: (0, 0)),
        ],
        out_specs=pl.BlockSpec((tm, D_MODEL), lambda i, e: (i, 0)),
        out_shape=jax.ShapeDtypeStruct((t, D_MODEL), F32),
        scratch_shapes=[pltpu.VMEM((tm, D_MODEL), F32)],
        compiler_params=_cparams(("parallel", "arbitrary")),
        name="moe_experts",
    )(xn, comb, h, wgu, wd, nfin)


def _forward(x, p):
    n_seq, seq_len, _ = x.shape
    t = n_seq * seq_len
    x2 = x.reshape(t, D_MODEL)
    proj, ab = _inproj(x2, p["norm_mix"], p["w_main"], p["w_ab"])

    q_r, k_r = _attn_prep(proj, p["cos"], p["sin"], p["norm_q_b"], p["norm_k_b"], seq_len)
    attn = _flash(q_r, k_r, proj, p["k_meta"], p["v_meta"], p["meta_bias"], n_seq, seq_len)

    zeros_halo = jnp.zeros((n_seq, HALO, CONV_CH), BF16)
    first_rows = proj.reshape(n_seq, seq_len, N_MAIN)[:, :HALO, :CONV_CH]
    meta_x = jnp.broadcast_to(p["meta_chunk_x"][None], (n_seq, CHUNK, CONV_CH)).reshape(n_seq * CHUNK, CONV_CH)
    meta_ab = jnp.broadcast_to(p["meta_chunk_ab"][None], (n_seq, CHUNK, LANES)).reshape(n_seq * CHUNK, LANES)
    conv_args = (p["conv_w"], p["a_log"], p["dt_bias"])
    mq, mk, mv, mgc, mgt = _conv(meta_x, 0, zeros_halo, first_rows, meta_ab, *conv_args, tm=CHUNK, pad_rows=PAD_A)
    m_tutw, m_qk = _pre(mq, mk, mgc, mgt)
    zero_state = jnp.zeros((n_seq, 1, A_V_HEADS, HEAD, HEAD), F32)
    _, s_meta = _rec(mq, mk, mv, m_tutw, m_qk, mgc, zero_state, n_seq, 1)

    tm = _tile(seq_len, 512)
    per_seq = seq_len // tm
    tiles = proj.reshape(n_seq, per_seq, tm, N_MAIN)
    tails = tiles[:, :, tm - HALO:, :CONV_CH]
    heads = tiles[:, :, :HALO, :CONV_CH]
    meta_tail = jnp.broadcast_to(p["meta_chunk_x"][None, None, CHUNK - HALO:], (n_seq, 1, HALO, CONV_CH))
    prev = jnp.concatenate([meta_tail, tails[:, :-1]], axis=1).reshape(n_seq * per_seq, HALO, CONV_CH)
    nxt = jnp.concatenate([heads[:, 1:], zeros_halo[:, None]], axis=1).reshape(n_seq * per_seq, HALO, CONV_CH)
    qh, kh, vh, gates_c, gates_t = _conv(proj, OFF_QKV // CONV_CH, prev, nxt, ab, *conv_args, tm=tm, pad_rows=0)
    tutw, qkm = _pre(qh, kh, gates_c, gates_t)
    s0 = jnp.concatenate([s_meta, zero_state], axis=1)
    o, _ = _rec(qh, kh, vh, tutw, qkm, gates_c, s0, n_seq, 2)

    h, xn, comb = _post(o, proj, attn, x2, p["norm_out_a"], p["w_branch_a"], p["w_branch_b"], p["w_out"],
                        p["norm_ffn"], p["w_route"], p["b_route"])
    y = _moe(xn, comb, h, p["w_gate_up"], p["w_down"], p["norm_final"])
    return y.reshape(n_seq, seq_len, D_MODEL)


def _prepare(meta_tokens, norm_mix, w_in, conv_w, a_log_fwd, a_log_bwd, dt_bias_fwd, dt_bias_bwd,
             norm_out_a, norm_q_b, norm_k_b, w_branch_a, w_branch_b, w_out, norm_ffn, w_group, b_group,
             w_router, b_router, w_gate_e, w_up_e, w_down_e, norm_final, max_len):
    w = w_in[0]
    bounds = np.cumsum([0, A_KEY_DIM, A_KEY_DIM, A_VAL_DIM, A_VAL_DIM, 4 * A_V_HEADS, B_Q_DIM, B_KV_DIM,
                        B_KV_DIM, D_MODEL, D_MODEL])
    qa, ka, va, za, abw, qb, kb, vb, ga, gb = [w[:, bounds[i]:bounds[i + 1]] for i in range(10)]
    p = {}
    p["w_main"] = jnp.concatenate([qa, ka, va, za, qb, ga, gb, kb, vb], axis=1).astype(BF16)
    src = _gate_source_columns()
    p["w_ab"] = jnp.pad(abw[:, src], ((0, 0), (0, LANES - GATE_LANES))).astype(BF16)
    p["norm_mix"] = norm_mix[0][None]
    p["conv_w"] = jnp.pad(conv_w[0], ((0, HALO - CONV_W), (0, 0)))
    pad_lanes = lambda v: jnp.pad(v, (0, LANES - v.shape[0]))[None]
    zeros_b = jnp.zeros((2 * A_V_HEADS,), F32)
    p["a_log"] = pad_lanes(jnp.concatenate([a_log_fwd[0], a_log_bwd[0], zeros_b])[src])
    p["dt_bias"] = pad_lanes(jnp.concatenate([dt_bias_fwd[0], dt_bias_bwd[0], zeros_b])[src])
    p["norm_out_a"] = norm_out_a[0][None]
    p["norm_q_b"] = norm_q_b[0][None]
    p["norm_k_b"] = norm_k_b[0][None]
    p["w_branch_a"] = w_branch_a[0].astype(BF16)
    p["w_branch_b"] = w_branch_b[0].astype(BF16)
    p["w_out"] = w_out[0].astype(BF16)
    p["norm_ffn"] = norm_ffn[0][None]
    p["w_route"] = jnp.pad(jnp.concatenate([w_router[0], w_group[0]], axis=1),
                           ((0, 0), (0, LANES - N_EXPERTS - N_GROUPS)))
    p["b_route"] = pad_lanes(jnp.concatenate([b_router[0], b_group[0]]))
    p["w_gate_up"] = jnp.concatenate([w_gate_e[0], w_up_e[0]], axis=-1).astype(BF16)
    p["w_down"] = w_down_e[0].astype(BF16)
    p["norm_final"] = norm_final[None]

    pos = jnp.arange(max_len, dtype=jnp.int32)
    p["cos"], p["sin"] = _rope_tables((pos // GRID_W).astype(F32), (pos % GRID_W).astype(F32))
    cos_m, sin_m = _rope_tables(jnp.full((N_META,), -1.0, F32), jnp.arange(N_META, dtype=F32))

    mproj, mab = _inproj(meta_tokens, p["norm_mix"], p["w_main"], p["w_ab"])
    _, k_meta = _attn_prep(mproj, cos_m, sin_m, p["norm_q_b"], p["norm_k_b"], N_META)
    p["k_meta"] = jnp.pad(k_meta, ((0, LANES - N_META), (0, 0)))
    p["v_meta"] = jnp.pad(mproj[:, OFF_VB:OFF_VB + B_KV_DIM], ((0, LANES - N_META), (0, 0)))
    p["meta_bias"] = jnp.where(jnp.arange(LANES) < N_META, 0.0, NEG_BIG).astype(F32)[None]
    p["meta_chunk_x"] = jnp.pad(mproj[:, :CONV_CH], ((PAD_A, 0), (0, 0)))
    p["meta_chunk_ab"] = jnp.pad(mab, ((PAD_A, 0), (0, 0)))
    return p


def kernel(x_prompt, x_sample, meta_tokens, norm_mix, w_in, conv_w, a_log_fwd, a_log_bwd, dt_bias_fwd,
           dt_bias_bwd, norm_out_a, norm_q_b, norm_k_b, w_branch_a, w_branch_b, w_out, norm_ffn, w_group,
           b_group, w_router, b_router, w_gate_e, w_up_e, w_down_e, norm_final):
    max_len = max(x_prompt.shape[1], x_sample.shape[1])
    p = _prepare(meta_tokens, norm_mix, w_in, conv_w, a_log_fwd, a_log_bwd, dt_bias_fwd, dt_bias_bwd,
                 norm_out_a, norm_q_b, norm_k_b, w_branch_a, w_branch_b, w_out, norm_ffn, w_group, b_group,
                 w_router, b_router, w_gate_e, w_up_e, w_down_e, norm_final, max_len)
    return (_forward(x_prompt, p), _forward(x_sample, p))
```

```python
import functools
import math

import numpy as np
import jax
import jax.numpy as jnp
from jax import lax
from jax.experimental import pallas as pl
from jax.experimental.pallas import tpu as pltpu

F32 = jnp.float32
BF16 = jnp.bfloat16

D_MODEL = 1024
N_META = 16
GRID_W = 64
EPS = 1e-6
HEAD = 128
A_QK_HEADS = 8
A_V_HEADS = 16
A_KEY_DIM = A_QK_HEADS * HEAD
A_VAL_DIM = A_V_HEADS * HEAD
CONV_W = 5
CONV_CH = 2 * A_KEY_DIM + A_VAL_DIM
CHUNK = 64
PAD_A = CHUNK - N_META
B_Q_HEADS = 8
B_KV_HEADS = 2
B_GROUPS = B_Q_HEADS // B_KV_HEADS
B_Q_DIM = B_Q_HEADS * HEAD
B_KV_DIM = B_KV_HEADS * HEAD
ROPE_THETA = 10000.0
N_GROUPS = 4
EXPERTS_PER_GROUP = 8
N_EXPERTS = N_GROUPS * EXPERTS_PER_GROUP
D_FF_EXPERT = 256

OFF_QKV = 0
OFF_Z = 4096
OFF_QB = 6144
OFF_GA = 7168
OFF_GB = 8192
OFF_KB = 9216
OFF_VB = 9472
N_MAIN = 9728
LANES = 128
GATE_LANES = 4 * A_V_HEADS
NEG_BIG = -1e30


def _gate_source_columns():
    src = []
    for lane in range(GATE_LANES):
        hq, kind, d, jj = lane // 8, (lane % 8) // 4, (lane % 4) // 2, lane % 2
        src.append(2 * A_V_HEADS * kind + A_V_HEADS * d + 2 * hq + jj)
    return np.asarray(src)
VMEM_LIMIT = 56 * 1024 * 1024


def _cparams(sem):
    return pltpu.CompilerParams(dimension_semantics=sem, vmem_limit_bytes=VMEM_LIMIT)


def _dot(a, b):
    return jnp.dot(a, b, preferred_element_type=F32)


def _dot_nt(a, b):
    return lax.dot_general(a, b, (((1,), (1,)), ((), ())), preferred_element_type=F32)


def _dot_tn(a, b):
    return lax.dot_general(a, b, (((0,), (0,)), ((), ())), preferred_element_type=F32)


def _dot_f32(a, b):
    return jnp.dot(a, b, preferred_element_type=F32, precision=lax.Precision.HIGHEST)


def _tile(n, pref):
    t = min(n, pref)
    assert n % t == 0, (n, t)
    return t


def _inproj_kernel(x_ref, g_ref, w_ref, wab_ref, o_ref, oab_ref, xn_ref):
    @pl.when(pl.program_id(1) == 0)
    def _():
        x = x_ref[...]
        xn = x * lax.rsqrt(jnp.mean(x * x, axis=-1, keepdims=True) + EPS) * g_ref[...]
        xb = xn.astype(BF16)
        xn_ref[...] = xb
        oab_ref[...] = _dot(xb, wab_ref[...])

    o_ref[...] = _dot(xn_ref[...], w_ref[...]).astype(o_ref.dtype)


def _inproj(x2, gain, w_main, w_ab):
    t = x2.shape[0]
    tm = _tile(t, 1024)
    tn = N_MAIN // 4
    return pl.pallas_call(
        _inproj_kernel,
        grid=(t // tm, N_MAIN // tn),
        in_specs=[
            pl.BlockSpec((tm, D_MODEL), lambda i, j: (i, 0)),
            pl.BlockSpec((1, D_MODEL), lambda i, j: (0, 0)),
            pl.BlockSpec((D_MODEL, tn), lambda i, j: (0, j)),
            pl.BlockSpec((D_MODEL, LANES), lambda i, j: (0, 0)),
        ],
        out_specs=[
            pl.BlockSpec((tm, tn), lambda i, j: (i, j)),
            pl.BlockSpec((tm, LANES), lambda i, j: (i, 0)),
        ],
        out_shape=[jax.ShapeDtypeStruct((t, N_MAIN), BF16), jax.ShapeDtypeStruct((t, LANES), F32)],
        scratch_shapes=[pltpu.VMEM((tm, D_MODEL), BF16)],
        compiler_params=_cparams(("parallel", "arbitrary")),
        name="in_proj",
    )(x2, gain, w_main, w_ab)


def _rope_tables(row, col):
    half = HEAD // 2
    inv_freq = ROPE_THETA ** (-jnp.arange(0, half, 2, dtype=F32) / half)
    ang_r = row[:, None] * inv_freq
    ang_c = col[:, None] * inv_freq
    cos = jnp.concatenate([jnp.cos(ang_r)] * 2 + [jnp.cos(ang_c)] * 2, axis=-1)
    sin = jnp.concatenate([-jnp.sin(ang_r), jnp.sin(ang_r), -jnp.sin(ang_c), jnp.sin(ang_c)], axis=-1)
    return cos, sin


def _norm_rope(x, gain, cos, sin, scale):
    lane = lax.broadcasted_iota(jnp.int32, cos.shape, 1)
    low = (lane % (HEAD // 2)) < (HEAD // 4)
    x = x.astype(F32)
    xn = x * lax.rsqrt(jnp.mean(x * x, axis=-1, keepdims=True) + EPS) * gain
    partner = jnp.where(low, pltpu.roll(xn, HEAD - HEAD // 4, 1), pltpu.roll(xn, HEAD // 4, 1))
    return (xn * cos + partner * sin) * scale


def _attn_prep_kernel(q_ref, k_ref, v_ref, cos_ref, sin_ref, gq_ref, gk_ref, qo_ref, ko_ref, vo_ref, *, tq):
    cos = cos_ref[...]
    sin = sin_ref[...]
    tm = cos.shape[0]
    for h in range(B_Q_HEADS):
        sl = slice(h * HEAD, (h + 1) * HEAD)
        q_t = _norm_rope(q_ref[:, sl], gq_ref[...], cos, sin, HEAD ** -0.5 * math.log2(math.e)).T
        for n in range(tm // tq):
            qo_ref[n, sl, :] = q_t[:, n * tq:(n + 1) * tq].astype(qo_ref.dtype)
    for h in range(B_KV_HEADS):
        sl = slice(h * HEAD, (h + 1) * HEAD)
        ko_ref[:, sl] = _norm_rope(k_ref[:, sl], gk_ref[...], cos, sin, 1.0).astype(ko_ref.dtype)
        vo_ref[h] = v_ref[:, sl].astype(F32).T.astype(vo_ref.dtype)


def _attn_prep(proj, cos, sin, gq, gk, seq_len, tq, tk):
    t = proj.shape[0]
    tm = tk
    per_seq = seq_len // tm
    return pl.pallas_call(
        functools.partial(_attn_prep_kernel, tq=tq),
        grid=(t // tm,),
        in_specs=[
            pl.BlockSpec((tm, B_Q_DIM), lambda i: (i, OFF_QB // B_Q_DIM)),
            pl.BlockSpec((tm, B_KV_DIM), lambda i: (i, OFF_KB // B_KV_DIM)),
            pl.BlockSpec((tm, B_KV_DIM), lambda i: (i, OFF_VB // B_KV_DIM)),
            pl.BlockSpec((tm, HEAD), lambda i: (i % per_seq, 0)),
            pl.BlockSpec((tm, HEAD), lambda i: (i % per_seq, 0)),
            pl.BlockSpec((1, HEAD), lambda i: (0, 0)),
            pl.BlockSpec((1, HEAD), lambda i: (0, 0)),
        ],
        out_specs=[
            pl.BlockSpec((tm // tq, B_Q_DIM, tq), lambda i: (i, 0, 0)),
            pl.BlockSpec((tm, B_KV_DIM), lambda i: (i, 0)),
            pl.BlockSpec((B_KV_HEADS, None, HEAD, tm), lambda i: (0, i, 0, 0)),
        ],
        out_shape=[
            jax.ShapeDtypeStruct((t // tq, B_Q_DIM, tq), BF16),
            jax.ShapeDtypeStruct((t, B_KV_DIM), BF16),
            jax.ShapeDtypeStruct((B_KV_HEADS, t // tm, HEAD, tm), BF16),
        ],
        compiler_params=_cparams(("parallel",)),
        name="attn_prep",
    )(proj, proj, proj, cos, sin, gq, gk)


def _meta_key_kernel(k_ref, cos_ref, sin_ref, gk_ref, ko_ref):
    for h in range(B_KV_HEADS):
        sl = slice(h * HEAD, (h + 1) * HEAD)
        ko_ref[:, sl] = _norm_rope(k_ref[:, sl], gk_ref[...], cos_ref[...], sin_ref[...], 1.0).astype(ko_ref.dtype)


def _meta_keys(mproj, cos, sin, gk):
    return pl.pallas_call(
        _meta_key_kernel,
        grid=(1,),
        in_specs=[
            pl.BlockSpec((N_META, B_KV_DIM), lambda i: (0, OFF_KB // B_KV_DIM)),
            pl.BlockSpec((N_META, HEAD), lambda i: (0, 0)),
            pl.BlockSpec((N_META, HEAD), lambda i: (0, 0)),
            pl.BlockSpec((1, HEAD), lambda i: (0, 0)),
        ],
        out_specs=pl.BlockSpec((N_META, B_KV_DIM), lambda i: (0, 0)),
        out_shape=jax.ShapeDtypeStruct((N_META, B_KV_DIM), BF16),
        name="meta_keys",
    )(mproj, cos, sin, gk)


def _flash_kernel(qt_ref, k_ref, vt_ref, km_ref, vmt_ref, o_ref,
                  qa_ref, sa_ref, sb_ref, p_ref, m_ref, l_ref, acc_ref, *, tq, tk, nkb):
    cols = B_GROUPS * tq
    for g in range(B_GROUPS):
        qa_ref[:, g * tq:(g + 1) * tq] = qt_ref[g * HEAD:(g + 1) * HEAD, :]

    s = _dot(km_ref[...], qa_ref[...])
    key = lax.broadcasted_iota(jnp.int32, s.shape, 0)
    s = jnp.where(key < N_META, s, NEG_BIG)
    m = jnp.max(s, axis=0, keepdims=True)
    p = jnp.exp2(s - m)
    m_ref[...] = m
    l_ref[...] = jnp.sum(p, axis=0, keepdims=True)
    acc_ref[...] = _dot(vmt_ref[...], p.astype(BF16))

    def scores(s_ref, kb):
        start = pl.multiple_of(kb * tk, tk)
        s_ref[...] = _dot(k_ref[pl.ds(start, tk), :], qa_ref[...])

    def softmax_block(s_ref, kb):
        alphas = []
        for cb in range(cols // LANES):
            cs = slice(cb * LANES, (cb + 1) * LANES)
            s = s_ref[:, cs]
            m_old = m_ref[:, cs]
            m_new = jnp.maximum(m_old, jnp.max(s, axis=0, keepdims=True))
            alpha = jnp.exp2(m_old - m_new)
            p = jnp.exp2(s - m_new)
            m_ref[:, cs] = m_new
            l_ref[:, cs] = alpha * l_ref[:, cs] + jnp.sum(p, axis=0, keepdims=True)
            p_ref[:, cs] = p.astype(BF16)
            alphas.append(alpha)
        alpha_row = jnp.concatenate(alphas, axis=-1)
        acc_ref[...] = acc_ref[...] * alpha_row + _dot(vt_ref[kb], p_ref[...])

    scores(sa_ref, 0)
    if nkb > 1:
        def body(i, carry):
            scores(sb_ref, 2 * i + 1)
            softmax_block(sa_ref, 2 * i)
            scores(sa_ref, 2 * i + 2)
            softmax_block(sb_ref, 2 * i + 1)
            return carry

        lax.fori_loop(0, nkb // 2 - 1, body, 0)
        scores(sb_ref, nkb - 1)
        softmax_block(sa_ref, nkb - 2)
        softmax_block(sb_ref, nkb - 1)
    else:
        softmax_block(sa_ref, 0)
    out_t = acc_ref[...] * (1.0 / l_ref[...])
    for g in range(B_GROUPS):
        o_ref[:, g * HEAD:(g + 1) * HEAD] = out_t[:, g * tq:(g + 1) * tq].T.astype(o_ref.dtype)


def _flash_tiles(seq_len):
    tq = _tile(seq_len, 256)
    tk = _tile(seq_len, 512)
    assert seq_len // tk == 1 or (seq_len // tk) % 2 == 0
    return tq, tk


def _flash(q_t, k_r, v_t, km, vmt, n_seq, seq_len):
    tq, tk = _flash_tiles(seq_len)
    nkb = seq_len // tk
    ntq = seq_len // tq
    cols = B_GROUPS * tq
    gw = B_GROUPS * HEAD
    k3 = k_r.reshape(n_seq, seq_len, B_KV_DIM)
    out = pl.pallas_call(
        functools.partial(_flash_kernel, tq=tq, tk=tk, nkb=nkb),
        grid=(n_seq, B_KV_HEADS, ntq),
        in_specs=[
            pl.BlockSpec((None, gw, tq), lambda b, j, i: (b * ntq + i, j, 0)),
            pl.BlockSpec((None, seq_len, HEAD), lambda b, j, i: (b, 0, j)),
            pl.BlockSpec((None, nkb, HEAD, tk), lambda b, j, i: (j, b, 0, 0)),
            pl.BlockSpec((LANES, HEAD), lambda b, j, i: (0, j)),
            pl.BlockSpec((None, HEAD, LANES), lambda b, j, i: (j, 0, 0)),
        ],
        out_specs=pl.BlockSpec((None, tq, gw), lambda b, j, i: (b, i, j)),
        out_shape=jax.ShapeDtypeStruct((n_seq, seq_len, B_Q_DIM), BF16),
        scratch_shapes=[
            pltpu.VMEM((HEAD, cols), BF16),
            pltpu.VMEM((tk, cols), F32),
            pltpu.VMEM((tk, cols), F32),
            pltpu.VMEM((tk, cols), BF16),
            pltpu.VMEM((1, cols), F32),
            pltpu.VMEM((1, cols), F32),
            pltpu.VMEM((HEAD, cols), F32),
        ],
        compiler_params=_cparams(("parallel", "parallel", "arbitrary")),
        name="flash_attn",
    )(q_t, k3, v_t, km, vmt)
    return out.reshape(n_seq * seq_len, B_Q_DIM)


CONV_CB = 512
HALO = 8


def _conv_kernel(x_ref, prev_ref, next_ref, ab_ref, cw_ref, alog_ref, dtb_ref,
                 q_ref, k_ref, v_ref, gc_ref, gt_ref, xs_ref, *, tm, pad_rows):
    row = lax.broadcasted_iota(jnp.int32, (tm, 1), 0)
    valid = (row % CHUNK) >= pad_rows
    for cb in range(CONV_CH // CONV_CB):
        cs = slice(cb * CONV_CB, (cb + 1) * CONV_CB)
        xs_ref[0:HALO, :] = prev_ref[0, :, cs].astype(F32)
        xs_ref[HALO:HALO + tm, :] = x_ref[:, cs].astype(F32)
        xs_ref[HALO + tm:, :] = next_ref[0, :, cs].astype(F32)
        y = jnp.zeros((tm, CONV_CB), F32)
        for w in range(CONV_W):
            y = y + xs_ref[pl.ds(HALO - CONV_W // 2 + w, tm), :] * cw_ref[w:w + 1, cs]
        y = y * (1.0 / (1.0 + jnp.exp(-y)))
        if pad_rows:
            y = jnp.where(valid, y, 0.0)
        for hh in range(CONV_CB // HEAD):
            col = cb * CONV_CB + hh * HEAD
            yh = y[:, hh * HEAD:(hh + 1) * HEAD]
            if col < 2 * A_KEY_DIM:
                yh = yh * lax.rsqrt(jnp.sum(yh * yh, axis=-1, keepdims=True) + EPS)
            if col < A_KEY_DIM:
                q_ref[col // HEAD] = (yh * (HEAD ** -0.5)).astype(q_ref.dtype)
            elif col < 2 * A_KEY_DIM:
                k_ref[(col - A_KEY_DIM) // HEAD] = yh.astype(k_ref.dtype)
            else:
                v_ref[(col - 2 * A_KEY_DIM) // HEAD] = yh.astype(v_ref.dtype)

    ab = ab_ref[...]
    lane = lax.broadcasted_iota(jnp.int32, (tm, LANES), 1)
    is_gate = lane < GATE_LANES
    is_a = is_gate & ((lane % 8) < 4)
    z = ab + dtb_ref[...]
    softplus = jnp.maximum(z, 0.0) + jnp.log1p(jnp.exp(-jnp.abs(z)))
    g = -jnp.exp(alog_ref[...]) * softplus
    beta = 1.0 / (1.0 + jnp.exp(-ab))
    g = jnp.where(valid & is_a, g, 0.0)
    beta = jnp.where(valid & is_gate & (~is_a), beta, 0.0)

    r = lax.broadcasted_iota(jnp.int32, (CHUNK, CHUNK), 0)
    c = lax.broadcasted_iota(jnp.int32, (CHUNK, CHUNK), 1)
    lower = (c <= r).astype(F32)
    upper = (c >= r).astype(F32)
    ones = jnp.ones((CHUNK, CHUNK), F32)
    lane_c = lax.broadcasted_iota(jnp.int32, (CHUNK, LANES), 1)
    fwd_lane = (lane_c % 4) < 2
    a_lane = (lane_c % 8) < 4
    for ck in range(tm // CHUNK):
        rs = slice(ck * CHUNK, (ck + 1) * CHUNK)
        gk = g[rs]
        gcum = jnp.where(fwd_lane, _dot_f32(lower, gk), _dot_f32(upper, gk))
        gtot = _dot_f32(ones, gk)
        e = jnp.exp(gcum)
        dec = jnp.exp(gtot - gcum)
        eg = jnp.exp(gtot)
        bk = beta[rs]
        t0 = jnp.where(a_lane, gcum, bk)
        t1 = jnp.where(a_lane, e, bk * pltpu.roll(e, 4, 1))
        t2 = jnp.where(a_lane, dec, pltpu.roll(eg, 4, 1))
        gc_ref[rs, 0:LANES] = t0
        gc_ref[rs, LANES:2 * LANES] = t1
        gc_ref[rs, 2 * LANES:3 * LANES] = t2
        gt_ref[0, 0:LANES, rs] = t0.T
        gt_ref[0, LANES:2 * LANES, rs] = t1.T
        gt_ref[0, 2 * LANES:3 * LANES, rs] = t2.T


def _conv(x, col_block, prev, nxt, ab, cw, alog, dtb, tm, pad_rows):
    t = x.shape[0]
    nt = t // tm
    return pl.pallas_call(
        functools.partial(_conv_kernel, tm=tm, pad_rows=pad_rows),
        grid=(nt,),
        in_specs=[
            pl.BlockSpec((tm, CONV_CH), lambda i: (i, col_block)),
            pl.BlockSpec((1, HALO, CONV_CH), lambda i: (i, 0, 0)),
            pl.BlockSpec((1, HALO, CONV_CH), lambda i: (i, 0, 0)),
            pl.BlockSpec((tm, LANES), lambda i: (i, 0)),
            pl.BlockSpec((HALO, CONV_CH), lambda i: (0, 0)),
            pl.BlockSpec((1, LANES), lambda i: (0, 0)),
            pl.BlockSpec((1, LANES), lambda i: (0, 0)),
        ],
        out_specs=[
            pl.BlockSpec((A_QK_HEADS, tm, HEAD), lambda i: (0, i, 0)),
            pl.BlockSpec((A_QK_HEADS, tm, HEAD), lambda i: (0, i, 0)),
            pl.BlockSpec((A_V_HEADS, tm, HEAD), lambda i: (0, i, 0)),
            pl.BlockSpec((tm, 3 * LANES), lambda i: (i, 0)),
            pl.BlockSpec((1, 3 * LANES, tm), lambda i: (i, 0, 0)),
        ],
        out_shape=[
            jax.ShapeDtypeStruct((A_QK_HEADS, t, HEAD), BF16),
            jax.ShapeDtypeStruct((A_QK_HEADS, t, HEAD), BF16),
            jax.ShapeDtypeStruct((A_V_HEADS, t, HEAD), BF16),
            jax.ShapeDtypeStruct((t, 3 * LANES), F32),
            jax.ShapeDtypeStruct((nt, 3 * LANES, tm), F32),
        ],
        scratch_shapes=[pltpu.VMEM((tm + 2 * HALO, CONV_CB), F32)],
        compiler_params=_cparams(("parallel",)),
        name="conv_gates",
    )(x, prev, nxt, ab, cw, alog, dtb)


def _pre_kernel(q_ref, k_ref, gc_ref, gt_ref, tutw_ref, qk_ref, *, nck):
    hq = pl.program_id(1)
    r = lax.broadcasted_iota(jnp.int32, (CHUNK, LANES), 0)
    lane = lax.broadcasted_iota(jnp.int32, (CHUNK, LANES), 1)
    c = lane & (CHUNK - 1)
    left = lane < CHUNK

    def same(k):
        return (r >> k) == (c >> k)

    eye = jnp.where(r == c, 1.0, 0.0)
    same1 = same(1)
    levels = []
    k = 1
    while (1 << k) < CHUNK:
        levels.append(same(k + 1) & (~same(k)))
        k += 1

    def blockdiag(m):
        mb = m.astype(BF16)
        zero = jnp.zeros_like(mb)
        return jnp.concatenate([jnp.where(left, mb, zero), jnp.where(left, zero, mb)], axis=0)

    def pair_cols(x, rs, i):
        return jnp.where(left, x[rs, i:i + 1], x[rs, i + 1:i + 2])

    def pair_rows(x, i):
        return jnp.concatenate([x[i:i + 1], x[i + 1:i + 2]], axis=-1)

    g0 = pltpu.roll(gc_ref[:, 0:LANES], lax.rem(LANES - 8 * hq, LANES), 1)
    row0 = pl.multiple_of(8 * hq, 8)
    chains = []
    for ck in range(nck):
        rs = slice(ck * CHUNK, (ck + 1) * CHUNK)
        qb = q_ref[rs, :]
        kb = k_ref[rs, :]
        k2 = jnp.concatenate([kb, kb], axis=0)
        kk = _dot_nt(kb, k2)
        qk = _dot_nt(qb, k2)
        rows0 = gt_ref[pl.ds(row0, 8), rs]
        rows1 = gt_ref[pl.ds(LANES + row0, 8), rs]
        for d in range(2):
            incl = (c <= r) if d == 0 else (c >= r)
            strict = (c < r) if d == 0 else (c > r)
            gl = 2 * d
            gdiff = pair_cols(g0, rs, gl) - pair_rows(rows0, gl)
            decay = jnp.where(incl, jnp.exp(jnp.where(incl, gdiff, 0.0)), 0.0)
            qk_ref[d, rs, :] = (qk * decay).astype(qk_ref.dtype)
            a = jnp.where(strict, kk * decay, 0.0) * pair_cols(g0, rs, 4 + gl)
            chains.append(dict(a=a, t=eye - jnp.where(same1, a, 0.0), d=d, rs=rs,
                               beta=pair_rows(rows0, 4 + gl), be=pair_rows(rows1, 4 + gl)))
    for mask in levels:
        xs = [_dot(jnp.where(mask, ch["a"], 0.0).astype(BF16), blockdiag(ch["t"])) for ch in chains]
        ys = [_dot(ch["t"].astype(BF16), blockdiag(x)) for ch, x in zip(chains, xs)]
        for ch, y in zip(chains, ys):
            ch["t"] = ch["t"] - y
    for ch in chains:
        tu = ch["t"] * ch["beta"]
        tw = ch["t"] * ch["be"]
        d, rs = ch["d"], ch["rs"]
        tutw_ref[d, 0, rs, :] = jnp.where(left, tu, pltpu.roll(tw, CHUNK, 1)).astype(tutw_ref.dtype)
        tutw_ref[d, 1, rs, :] = jnp.where(left, pltpu.roll(tu, CHUNK, 1), tw).astype(tutw_ref.dtype)


def _pre(qh, kh, gates_c, gates_t):
    t = qh.shape[1]
    ct = gates_t.shape[2]
    return pl.pallas_call(
        functools.partial(_pre_kernel, nck=ct // CHUNK),
        grid=(t // ct, A_QK_HEADS),
        in_specs=[
            pl.BlockSpec((None, ct, HEAD), lambda i, h: (h, i, 0)),
            pl.BlockSpec((None, ct, HEAD), lambda i, h: (h, i, 0)),
            pl.BlockSpec((ct, 3 * LANES), lambda i, h: (i, 0)),
            pl.BlockSpec((None, 3 * LANES, ct), lambda i, h: (i, 0, 0)),
        ],
        out_specs=[
            pl.BlockSpec((2, 2, ct, HEAD), lambda i, h: (0, h, i, 0)),
            pl.BlockSpec((2, None, ct, HEAD), lambda i, h: (0, h, i, 0)),
        ],
        out_shape=[
            jax.ShapeDtypeStruct((2, A_V_HEADS, t, HEAD), BF16),
            jax.ShapeDtypeStruct((2, A_QK_HEADS, t, HEAD), BF16),
        ],
        compiler_params=_cparams(("parallel", "arbitrary")),
        name="chunk_pre",
    )(qh, kh, gates_c, gates_t)


def _rec_kernel(q_ref, k_ref, v_ref, tutw_ref, qk_ref, g_ref, s0_ref, o_ref, sfin_ref, s_ref, *, nc):
    d = pl.program_id(1)
    cidx = pl.program_id(2)

    @pl.when(cidx == 0)
    def _():
        s_ref[...] = s0_ref[...]

    def dir_lanes(x):
        return jnp.where(d == 0, x, pltpu.roll(x, LANES - 2, 1))

    g1 = dir_lanes(g_ref[:, LANES:2 * LANES])
    g2 = dir_lanes(g_ref[:, 2 * LANES:3 * LANES])
    heads = range(A_V_HEADS)
    qk_lhs = [jnp.concatenate([q_ref[hq], k_ref[hq]], axis=0) for hq in range(A_QK_HEADS)]
    k_t = [k_ref[hq].astype(F32).T.astype(BF16) for hq in range(A_QK_HEADS)]
    states = [s_ref[j] for j in heads]
    qs_ks = [_dot(qk_lhs[j // 2], states[j].astype(BF16)) for j in heads]
    v_new = [_dot(tutw_ref[j], jnp.concatenate([v_ref[j], (-qs_ks[j][CHUNK:]).astype(BF16)], axis=0))
             for j in heads]
    for j in heads:
        hq, jj = divmod(j, 2)
        gl = 8 * hq + jj
        qkm = qk_ref[hq][:, jj * CHUNK:(jj + 1) * CHUNK]
        o = g1[:, gl:gl + 1] * qs_ks[j][:CHUNK] + _dot(qkm, v_new[j].astype(BF16))
        o_ref[j] = o.astype(o_ref.dtype)
    for j in heads:
        hq, jj = divmod(j, 2)
        gl = 8 * hq + jj
        v_dec = (v_new[j] * g2[:, gl:gl + 1]).astype(BF16)
        s_ref[j] = states[j] * g2[0:1, 4 + gl:5 + gl] + _dot(k_t[hq], v_dec)

    @pl.when(cidx == nc - 1)
    def _():
        sfin_ref[...] = s_ref[...]


def _rec(qh, kh, vh, tutw, qkm, gates_c, s0, n_seq, n_dir):
    t = qh.shape[1]
    nc = t // n_seq // CHUNK

    def chunk(b, d, c):
        return b * nc + c + d * (nc - 1 - 2 * c)

    return pl.pallas_call(
        functools.partial(_rec_kernel, nc=nc),
        grid=(n_seq, n_dir, nc),
        in_specs=[
            pl.BlockSpec((A_QK_HEADS, CHUNK, HEAD), lambda b, d, c: (0, chunk(b, d, c), 0)),
            pl.BlockSpec((A_QK_HEADS, CHUNK, HEAD), lambda b, d, c: (0, chunk(b, d, c), 0)),
            pl.BlockSpec((A_V_HEADS, CHUNK, HEAD), lambda b, d, c: (0, chunk(b, d, c), 0)),
            pl.BlockSpec((None, A_V_HEADS, CHUNK, HEAD), lambda b, d, c: (d, 0, chunk(b, d, c), 0)),
            pl.BlockSpec((None, A_QK_HEADS, CHUNK, HEAD), lambda b, d, c: (d, 0, chunk(b, d, c), 0)),
            pl.BlockSpec((CHUNK, 3 * LANES), lambda b, d, c: (chunk(b, d, c), 0)),
            pl.BlockSpec((None, None, A_V_HEADS, HEAD, HEAD), lambda b, d, c: (b, d, 0, 0, 0)),
        ],
        out_specs=[
            pl.BlockSpec((None, A_V_HEADS, CHUNK, HEAD), lambda b, d, c: (d, 0, chunk(b, d, c), 0)),
            pl.BlockSpec((None, None, A_V_HEADS, HEAD, HEAD), lambda b, d, c: (b, d, 0, 0, 0)),
        ],
        out_shape=[
            jax.ShapeDtypeStruct((n_dir, A_V_HEADS, t, HEAD), F32),
            jax.ShapeDtypeStruct((n_seq, n_dir, A_V_HEADS, HEAD, HEAD), F32),
        ],
        scratch_shapes=[pltpu.VMEM((A_V_HEADS, HEAD, HEAD), F32)],
        compiler_params=_cparams(("parallel", "parallel", "arbitrary")),
        name="chunk_rec",
    )(qh, kh, vh, tutw, qkm, gates_c, s0)


def _post_kernel(o_ref, z_ref, attn_ref, ga_ref, gb_ref, x_ref, na_ref, wa_ref, wb_ref, wo_ref,
                 nf_ref, wr_ref, br_ref, h_ref, xn_ref, comb_ref):
    parts = []
    for j in range(A_V_HEADS):
        o = o_ref[0, j] + o_ref[1, j]
        on = o * lax.rsqrt(jnp.mean(o * o, axis=-1, keepdims=True) + EPS) * na_ref[...]
        z = z_ref[:, j * HEAD:(j + 1) * HEAD].astype(F32)
        parts.append((on * (z * (1.0 / (1.0 + jnp.exp(-z))))).astype(BF16))
    ya = _dot(jnp.concatenate(parts, axis=-1), wa_ref[...])
    yb = _dot(attn_ref[...], wb_ref[...])
    sig_a = 1.0 / (1.0 + jnp.exp(-ga_ref[...].astype(F32)))
    sig_b = 1.0 / (1.0 + jnp.exp(-gb_ref[...].astype(F32)))
    merged = sig_a * ya + sig_b * yb
    h = x_ref[...] + _dot(merged.astype(BF16), wo_ref[...])
    h_ref[...] = h
    xn = h * lax.rsqrt(jnp.mean(h * h, axis=-1, keepdims=True) + EPS) * nf_ref[...]
    xn_ref[...] = xn.astype(xn_ref.dtype)

    logits = _dot_f32(xn, wr_ref[...]) + br_ref[...]
    lane = lax.broadcasted_iota(jnp.int32, logits.shape, 1)
    lane_f = lane.astype(F32)
    g_mask = (lane >= N_EXPERTS) & (lane < N_EXPERTS + N_GROUPS)
    gl = jnp.where(g_mask, logits, NEG_BIG)
    g_max = jnp.max(gl, axis=-1, keepdims=True)
    g_w = 1.0 / jnp.sum(jnp.exp(gl - g_max), axis=-1, keepdims=True)
    g_idx = jnp.min(jnp.where(gl == g_max, lane_f, 1e9), axis=-1, keepdims=True) - N_EXPERTS
    e_mask = (lane < N_EXPERTS) & ((lane // EXPERTS_PER_GROUP).astype(F32) == g_idx)
    el = jnp.where(e_mask, logits, NEG_BIG)
    m1 = jnp.max(el, axis=-1, keepdims=True)
    i1 = jnp.min(jnp.where(el == m1, lane_f, 1e9), axis=-1, keepdims=True)
    el2 = jnp.where(lane_f == i1, NEG_BIG, el)
    m2 = jnp.max(el2, axis=-1, keepdims=True)
    i2 = jnp.min(jnp.where(el2 == m2, lane_f, 1e9), axis=-1, keepdims=True)
    e2 = jnp.exp(m2 - m1)
    w1 = g_w / (1.0 + e2)
    w2 = g_w * e2 / (1.0 + e2)
    comb_ref[...] = jnp.where(lane_f == i1, w1, 0.0) + jnp.where(lane_f == i2, w2, 0.0)


def _post(o, proj, attn, x2, na, wa, wb, wo, nf, wr, br):
    t = x2.shape[0]
    tm = _tile(t, 256)
    const = lambda i: (0, 0)
    return pl.pallas_call(
        _post_kernel,
        grid=(t // tm,),
        in_specs=[
            pl.BlockSpec((2, A_V_HEADS, tm, HEAD), lambda i: (0, 0, i, 0)),
            pl.BlockSpec((tm, A_VAL_DIM), lambda i: (i, OFF_Z // A_VAL_DIM)),
            pl.BlockSpec((tm, B_Q_DIM), lambda i: (i, 0)),
            pl.BlockSpec((tm, D_MODEL), lambda i: (i, OFF_GA // D_MODEL)),
            pl.BlockSpec((tm, D_MODEL), lambda i: (i, OFF_GB // D_MODEL)),
            pl.BlockSpec((tm, D_MODEL), lambda i: (i, 0)),
            pl.BlockSpec((1, HEAD), const),
            pl.BlockSpec((A_VAL_DIM, D_MODEL), const),
            pl.BlockSpec((B_Q_DIM, D_MODEL), const),
            pl.BlockSpec((D_MODEL, D_MODEL), const),
            pl.BlockSpec((1, D_MODEL), const),
            pl.BlockSpec((D_MODEL, LANES), const),
            pl.BlockSpec((1, LANES), const),
        ],
        out_specs=[
            pl.BlockSpec((tm, D_MODEL), lambda i: (i, 0)),
            pl.BlockSpec((tm, D_MODEL), lambda i: (i, 0)),
            pl.BlockSpec((tm, LANES), lambda i: (i, 0)),
        ],
        out_shape=[
            jax.ShapeDtypeStruct((t, D_MODEL), F32),
            jax.ShapeDtypeStruct((t, D_MODEL), BF16),
            jax.ShapeDtypeStruct((t, LANES), F32),
        ],
        compiler_params=_cparams(("parallel",)),
        name="post_mixer",
    )(o, proj, attn, proj, proj, x2, na, wa, wb, wo, nf, wr, br)


def _moe_kernel(xn_ref, comb_ref, h_ref, wgu_ref, wd_ref, nfin_ref, y_ref):
    e = pl.program_id(1)
    acc_ref = y_ref

    @pl.when(e == 0)
    def _():
        acc_ref[...] = h_ref[...]

    gu = _dot(xn_ref[...], wgu_ref[...])
    gate = gu[:, :D_FF_EXPERT]
    up = gu[:, D_FF_EXPERT:]
    lane = lax.broadcasted_iota(jnp.int32, comb_ref.shape, 1)
    w = jnp.sum(jnp.where(lane == e, comb_ref[...], 0.0), axis=-1, keepdims=True)
    hid = gate * (1.0 / (1.0 + jnp.exp(-gate))) * up
    acc_ref[...] += w * _dot(hid.astype(BF16), wd_ref[...])

    @pl.when(e == N_EXPERTS - 1)
    def _():
        a = acc_ref[...]
        y_ref[...] = a * lax.rsqrt(jnp.mean(a * a, axis=-1, keepdims=True) + EPS) * nfin_ref[...]


def _moe(xn, comb, h, wgu, wd, nfin):
    t = xn.shape[0]
    tm = _tile(t, 2048)
    return pl.pallas_call(
        _moe_kernel,
        grid=(t // tm, N_EXPERTS),
        in_specs=[
            pl.BlockSpec((tm, D_MODEL), lambda i, e: (i, 0)),
            pl.BlockSpec((tm, LANES), lambda i, e: (i, 0)),
            pl.BlockSpec((tm, D_MODEL), lambda i, e: (i, 0)),
            pl.BlockSpec((None, D_MODEL, 2 * D_FF_EXPERT), lambda i, e: (e, 0, 0)),
            pl.BlockSpec((None, D_FF_EXPERT, D_MODEL), lambda i, e: (e, 0, 0)),
            pl.BlockSpec((1, D_MODEL), lambda i, e: (0, 0)),
        ],
        out_specs=pl.BlockSpec((tm, D_MODEL), lambda i, e: (i, 0)),
        out_shape=jax.ShapeDtypeStruct((t, D_MODEL), F32),
        compiler_params=_cparams(("parallel", "arbitrary")),
        name="moe_experts",
    )(xn, comb, h, wgu, wd, nfin)


def _forward(x, p):
    n_seq, seq_len, _ = x.shape
    t = n_seq * seq_len
    x2 = x.reshape(t, D_MODEL)
    proj, ab = _inproj(x2, p["norm_mix"], p["w_main"], p["w_ab"])

    tq, tk = _flash_tiles(seq_len)
    q_t, k_r, v_t = _attn_prep(proj, p["cos"], p["sin"], p["norm_q_b"], p["norm_k_b"], seq_len, tq, tk)
    attn = _flash(q_t, k_r, v_t, p["k_meta"], p["v_meta_t"], n_seq, seq_len)

    zeros_halo = jnp.zeros((n_seq, HALO, CONV_CH), BF16)
    first_rows = proj.reshape(n_seq, seq_len, N_MAIN)[:, :HALO, :CONV_CH]
    meta_x = jnp.broadcast_to(p["meta_chunk_x"][None], (n_seq, CHUNK, CONV_CH)).reshape(n_seq * CHUNK, CONV_CH)
    meta_ab = jnp.broadcast_to(p["meta_chunk_ab"][None], (n_seq, CHUNK, LANES)).reshape(n_seq * CHUNK, LANES)
    conv_args = (p["conv_w"], p["a_log"], p["dt_bias"])
    mq, mk, mv, mgc, mgt = _conv(meta_x, 0, zeros_halo, first_rows, meta_ab, *conv_args, tm=CHUNK, pad_rows=PAD_A)
    m_tutw, m_qk = _pre(mq, mk, mgc, mgt)
    zero_state = jnp.zeros((n_seq, 1, A_V_HEADS, HEAD, HEAD), F32)
    _, s_meta = _rec(mq, mk, mv, m_tutw, m_qk, mgc, zero_state, n_seq, 1)

    tm = _tile(seq_len, 512)
    per_seq = seq_len // tm
    tiles = proj.reshape(n_seq, per_seq, tm, N_MAIN)
    tails = tiles[:, :, tm - HALO:, :CONV_CH]
    heads = tiles[:, :, :HALO, :CONV_CH]
    meta_tail = jnp.broadcast_to(p["meta_chunk_x"][None, None, CHUNK - HALO:], (n_seq, 1, HALO, CONV_CH))
    prev = jnp.concatenate([meta_tail, tails[:, :-1]], axis=1).reshape(n_seq * per_seq, HALO, CONV_CH)
    nxt = jnp.concatenate([heads[:, 1:], zeros_halo[:, None]], axis=1).reshape(n_seq * per_seq, HALO, CONV_CH)
    qh, kh, vh, gates_c, gates_t = _conv(proj, OFF_QKV // CONV_CH, prev, nxt, ab, *conv_args, tm=tm, pad_rows=0)
    tutw, qkm = _pre(qh, kh, gates_c, gates_t)
    s0 = jnp.concatenate([s_meta, zero_state], axis=1)
    o, _ = _rec(qh, kh, vh, tutw, qkm, gates_c, s0, n_seq, 2)

    h, xn, comb = _post(o, proj, attn, x2, p["norm_out_a"], p["w_branch_a"], p["w_branch_b"], p["w_out"],
                        p["norm_ffn"], p["w_route"], p["b_route"])
    y = _moe(xn, comb, h, p["w_gate_up"], p["w_down"], p["norm_final"])
    return y.reshape(n_seq, seq_len, D_MODEL)


def _prepare(meta_tokens, norm_mix, w_in, conv_w, a_log_fwd, a_log_bwd, dt_bias_fwd, dt_bias_bwd,
             norm_out_a, norm_q_b, norm_k_b, w_branch_a, w_branch_b, w_out, norm_ffn, w_group, b_group,
             w_router, b_router, w_gate_e, w_up_e, w_down_e, norm_final, max_len):
    w = w_in[0]
    bounds = np.cumsum([0, A_KEY_DIM, A_KEY_DIM, A_VAL_DIM, A_VAL_DIM, 4 * A_V_HEADS, B_Q_DIM, B_KV_DIM,
                        B_KV_DIM, D_MODEL, D_MODEL])
    qa, ka, va, za, abw, qb, kb, vb, ga, gb = [w[:, bounds[i]:bounds[i + 1]] for i in range(10)]
    p = {}
    p["w_main"] = jnp.concatenate([qa, ka, va, za, qb, ga, gb, kb, vb], axis=1).astype(BF16)
    src = _gate_source_columns()
    p["w_ab"] = jnp.pad(abw[:, src], ((0, 0), (0, LANES - GATE_LANES))).astype(BF16)
    p["norm_mix"] = norm_mix[0][None]
    p["conv_w"] = jnp.pad(conv_w[0], ((0, HALO - CONV_W), (0, 0)))
    pad_lanes = lambda v: jnp.pad(v, (0, LANES - v.shape[0]))[None]
    zeros_b = jnp.zeros((2 * A_V_HEADS,), F32)
    p["a_log"] = pad_lanes(jnp.concatenate([a_log_fwd[0], a_log_bwd[0], zeros_b])[src])
    p["dt_bias"] = pad_lanes(jnp.concatenate([dt_bias_fwd[0], dt_bias_bwd[0], zeros_b])[src])
    p["norm_out_a"] = norm_out_a[0][None]
    p["norm_q_b"] = norm_q_b[0][None]
    p["norm_k_b"] = norm_k_b[0][None]
    p["w_branch_a"] = w_branch_a[0].astype(BF16)
    p["w_branch_b"] = w_branch_b[0].astype(BF16)
    p["w_out"] = w_out[0].astype(BF16)
    p["norm_ffn"] = norm_ffn[0][None]
    p["w_route"] = jnp.pad(jnp.concatenate([w_router[0], w_group[0]], axis=1),
                           ((0, 0), (0, LANES - N_EXPERTS - N_GROUPS)))
    p["b_route"] = pad_lanes(jnp.concatenate([b_router[0], b_group[0]]))
    p["w_gate_up"] = jnp.concatenate([w_gate_e[0], w_up_e[0]], axis=-1).astype(BF16)
    p["w_down"] = w_down_e[0].astype(BF16)
    p["norm_final"] = norm_final[None]

    pos = jnp.arange(max_len, dtype=jnp.int32)
    p["cos"], p["sin"] = _rope_tables((pos // GRID_W).astype(F32), (pos % GRID_W).astype(F32))
    cos_m, sin_m = _rope_tables(jnp.full((N_META,), -1.0, F32), jnp.arange(N_META, dtype=F32))

    mproj, mab = _inproj(meta_tokens, p["norm_mix"], p["w_main"], p["w_ab"])
    k_meta = _meta_keys(mproj, cos_m, sin_m, p["norm_k_b"])
    p["k_meta"] = jnp.pad(k_meta, ((0, LANES - N_META), (0, 0)))
    v_meta = jnp.pad(mproj[:, OFF_VB:OFF_VB + B_KV_DIM], ((0, LANES - N_META), (0, 0)))
    p["v_meta_t"] = v_meta.reshape(LANES, B_KV_HEADS, HEAD).transpose(1, 2, 0)
    p["meta_chunk_x"] = jnp.pad(mproj[:, :CONV_CH], ((PAD_A, 0), (0, 0)))
    p["meta_chunk_ab"] = jnp.pad(mab, ((PAD_A, 0), (0, 0)))
    return p


def kernel(x_prompt, x_sample, meta_tokens, norm_mix, w_in, conv_w, a_log_fwd, a_log_bwd, dt_bias_fwd,
           dt_bias_bwd, norm_out_a, norm_q_b, norm_k_b, w_branch_a, w_branch_b, w_out, norm_ffn, w_group,
           b_group, w_router, b_router, w_gate_e, w_up_e, w_down_e, norm_final):
    max_len = max(x_prompt.shape[1], x_sample.shape[1])
    p = _prepare(meta_tokens, norm_mix, w_in, conv_w, a_log_fwd, a_log_bwd, dt_bias_fwd, dt_bias_bwd,
                 norm_out_a, norm_q_b, norm_k_b, w_branch_a, w_branch_b, w_out, norm_ffn, w_group, b_group,
                 w_router, b_router, w_gate_e, w_up_e, w_down_e, norm_final, max_len)
    return (_forward(x_prompt, p), _forward(x_sample, p))
```

```python
import functools
import math

import numpy as np
import jax
import jax.numpy as jnp
from jax import lax
from jax.experimental import pallas as pl
from jax.experimental.pallas import tpu as pltpu

F32 = jnp.float32
BF16 = jnp.bfloat16

D_MODEL = 1024
N_META = 16
GRID_W = 64
EPS = 1e-6
HEAD = 128
A_QK_HEADS = 8
A_V_HEADS = 16
A_KEY_DIM = A_QK_HEADS * HEAD
A_VAL_DIM = A_V_HEADS * HEAD
CONV_W = 5
CONV_CH = 2 * A_KEY_DIM + A_VAL_DIM
CHUNK = 64
PAD_A = CHUNK - N_META
B_Q_HEADS = 8
B_KV_HEADS = 2
B_GROUPS = B_Q_HEADS // B_KV_HEADS
B_Q_DIM = B_Q_HEADS * HEAD
B_KV_DIM = B_KV_HEADS * HEAD
ROPE_THETA = 10000.0
N_GROUPS = 4
EXPERTS_PER_GROUP = 8
N_EXPERTS = N_GROUPS * EXPERTS_PER_GROUP
D_FF_EXPERT = 256

OFF_QKV = 0
OFF_Z = 4096
OFF_QB = 6144
OFF_GA = 7168
OFF_GB = 8192
OFF_KB = 9216
OFF_VB = 9472
N_MAIN = 9728
LANES = 128
GATE_LANES = 4 * A_V_HEADS
NEG_BIG = -1e30


def _gate_source_columns():
    src = []
    for lane in range(GATE_LANES):
        hq, kind, d, jj = lane // 8, (lane % 8) // 4, (lane % 4) // 2, lane % 2
        src.append(2 * A_V_HEADS * kind + A_V_HEADS * d + 2 * hq + jj)
    return np.asarray(src)
VMEM_LIMIT = 56 * 1024 * 1024


def _cparams(sem):
    return pltpu.CompilerParams(dimension_semantics=sem, vmem_limit_bytes=VMEM_LIMIT)


def _dot(a, b):
    return jnp.dot(a, b, preferred_element_type=F32)


def _dot_nt(a, b):
    return lax.dot_general(a, b, (((1,), (1,)), ((), ())), preferred_element_type=F32)


def _dot_tn(a, b):
    return lax.dot_general(a, b, (((0,), (0,)), ((), ())), preferred_element_type=F32)


def _dot_f32(a, b):
    return jnp.dot(a, b, preferred_element_type=F32, precision=lax.Precision.HIGHEST)


def _tile(n, pref):
    t = min(n, pref)
    assert n % t == 0, (n, t)
    return t


def _inproj_kernel(x_ref, g_ref, w_ref, wab_ref, o_ref, oab_ref, xn_ref):
    @pl.when(pl.program_id(1) == 0)
    def _():
        x = x_ref[...]
        xn = x * lax.rsqrt(jnp.mean(x * x, axis=-1, keepdims=True) + EPS) * g_ref[...]
        xb = xn.astype(BF16)
        xn_ref[...] = xb
        oab_ref[...] = _dot(xb, wab_ref[...])

    o_ref[...] = _dot(xn_ref[...], w_ref[...]).astype(o_ref.dtype)


def _inproj(x2, gain, w_main, w_ab):
    t = x2.shape[0]
    tm = _tile(t, 1024)
    tn = N_MAIN // 4
    return pl.pallas_call(
        _inproj_kernel,
        grid=(t // tm, N_MAIN // tn),
        in_specs=[
            pl.BlockSpec((tm, D_MODEL), lambda i, j: (i, 0)),
            pl.BlockSpec((1, D_MODEL), lambda i, j: (0, 0)),
            pl.BlockSpec((D_MODEL, tn), lambda i, j: (0, j)),
            pl.BlockSpec((D_MODEL, LANES), lambda i, j: (0, 0)),
        ],
        out_specs=[
            pl.BlockSpec((tm, tn), lambda i, j: (i, j)),
            pl.BlockSpec((tm, LANES), lambda i, j: (i, 0)),
        ],
        out_shape=[jax.ShapeDtypeStruct((t, N_MAIN), BF16), jax.ShapeDtypeStruct((t, LANES), F32)],
        scratch_shapes=[pltpu.VMEM((tm, D_MODEL), BF16)],
        compiler_params=_cparams(("parallel", "arbitrary")),
        name="in_proj",
    )(x2, gain, w_main, w_ab)


def _rope_tables(row, col):
    half = HEAD // 2
    inv_freq = ROPE_THETA ** (-jnp.arange(0, half, 2, dtype=F32) / half)
    ang_r = row[:, None] * inv_freq
    ang_c = col[:, None] * inv_freq
    cos = jnp.concatenate([jnp.cos(ang_r)] * 2 + [jnp.cos(ang_c)] * 2, axis=-1)
    sin = jnp.concatenate([-jnp.sin(ang_r), jnp.sin(ang_r), -jnp.sin(ang_c), jnp.sin(ang_c)], axis=-1)
    return cos, sin


def _norm_rope(x, gain, cos, sin, scale):
    lane = lax.broadcasted_iota(jnp.int32, cos.shape, 1)
    low = (lane % (HEAD // 2)) < (HEAD // 4)
    x = x.astype(F32)
    xn = x * lax.rsqrt(jnp.mean(x * x, axis=-1, keepdims=True) + EPS) * gain
    partner = jnp.where(low, pltpu.roll(xn, HEAD - HEAD // 4, 1), pltpu.roll(xn, HEAD // 4, 1))
    return (xn * cos + partner * sin) * scale


def _attn_prep_kernel(q_ref, k_ref, v_ref, cos_ref, sin_ref, gq_ref, gk_ref, qo_ref, ko_ref, vo_ref, *, tq):
    cos = cos_ref[...]
    sin = sin_ref[...]
    tm = cos.shape[0]
    for h in range(B_Q_HEADS):
        sl = slice(h * HEAD, (h + 1) * HEAD)
        q_t = _norm_rope(q_ref[:, sl], gq_ref[...], cos, sin, HEAD ** -0.5 * math.log2(math.e)).T
        for n in range(tm // tq):
            qo_ref[n, sl, :] = q_t[:, n * tq:(n + 1) * tq].astype(qo_ref.dtype)
    for h in range(B_KV_HEADS):
        sl = slice(h * HEAD, (h + 1) * HEAD)
        ko_ref[:, sl] = _norm_rope(k_ref[:, sl], gk_ref[...], cos, sin, 1.0).astype(ko_ref.dtype)
        vo_ref[h] = v_ref[:, sl].astype(F32).T.astype(vo_ref.dtype)


def _attn_prep(proj, cos, sin, gq, gk, seq_len, tq, tk):
    t = proj.shape[0]
    tm = tk
    per_seq = seq_len // tm
    return pl.pallas_call(
        functools.partial(_attn_prep_kernel, tq=tq),
        grid=(t // tm,),
        in_specs=[
            pl.BlockSpec((tm, B_Q_DIM), lambda i: (i, OFF_QB // B_Q_DIM)),
            pl.BlockSpec((tm, B_KV_DIM), lambda i: (i, OFF_KB // B_KV_DIM)),
            pl.BlockSpec((tm, B_KV_DIM), lambda i: (i, OFF_VB // B_KV_DIM)),
            pl.BlockSpec((tm, HEAD), lambda i: (i % per_seq, 0)),
            pl.BlockSpec((tm, HEAD), lambda i: (i % per_seq, 0)),
            pl.BlockSpec((1, HEAD), lambda i: (0, 0)),
            pl.BlockSpec((1, HEAD), lambda i: (0, 0)),
        ],
        out_specs=[
            pl.BlockSpec((tm // tq, B_Q_DIM, tq), lambda i: (i, 0, 0)),
            pl.BlockSpec((tm, B_KV_DIM), lambda i: (i, 0)),
            pl.BlockSpec((B_KV_HEADS, None, HEAD, tm), lambda i: (0, i, 0, 0)),
        ],
        out_shape=[
            jax.ShapeDtypeStruct((t // tq, B_Q_DIM, tq), BF16),
            jax.ShapeDtypeStruct((t, B_KV_DIM), BF16),
            jax.ShapeDtypeStruct((B_KV_HEADS, t // tm, HEAD, tm), BF16),
        ],
        compiler_params=_cparams(("parallel",)),
        name="attn_prep",
    )(proj, proj, proj, cos, sin, gq, gk)


def _meta_key_kernel(k_ref, cos_ref, sin_ref, gk_ref, ko_ref):
    for h in range(B_KV_HEADS):
        sl = slice(h * HEAD, (h + 1) * HEAD)
        ko_ref[:, sl] = _norm_rope(k_ref[:, sl], gk_ref[...], cos_ref[...], sin_ref[...], 1.0).astype(ko_ref.dtype)


def _meta_keys(mproj, cos, sin, gk):
    return pl.pallas_call(
        _meta_key_kernel,
        grid=(1,),
        in_specs=[
            pl.BlockSpec((N_META, B_KV_DIM), lambda i: (0, OFF_KB // B_KV_DIM)),
            pl.BlockSpec((N_META, HEAD), lambda i: (0, 0)),
            pl.BlockSpec((N_META, HEAD), lambda i: (0, 0)),
            pl.BlockSpec((1, HEAD), lambda i: (0, 0)),
        ],
        out_specs=pl.BlockSpec((N_META, B_KV_DIM), lambda i: (0, 0)),
        out_shape=jax.ShapeDtypeStruct((N_META, B_KV_DIM), BF16),
        name="meta_keys",
    )(mproj, cos, sin, gk)


def _flash_kernel(qt_ref, k_ref, vt_ref, km_ref, vmt_ref, o_ref,
                  qa_ref, sa_ref, sb_ref, p_ref, m_ref, l_ref, acc_ref, *, tq, tk, nkb):
    cols = B_GROUPS * tq
    for g in range(B_GROUPS):
        qa_ref[:, g * tq:(g + 1) * tq] = qt_ref[g * HEAD:(g + 1) * HEAD, :]

    s = _dot(km_ref[...], qa_ref[...])
    key = lax.broadcasted_iota(jnp.int32, s.shape, 0)
    s = jnp.where(key < N_META, s, NEG_BIG)
    m = jnp.max(s, axis=0, keepdims=True)
    p = jnp.exp2(s - m)
    m_ref[...] = m
    l_ref[...] = jnp.sum(p, axis=0, keepdims=True)
    acc_ref[:, 0:cols] = _dot(vmt_ref[...], p.astype(BF16))

    def scores(s_ref, kb):
        start = pl.multiple_of(kb * tk, tk)
        s_ref[:, 0:cols] = _dot(k_ref[pl.ds(start, tk), :], qa_ref[...])

    def softmax_block(s_ref, kb):
        alphas = []
        for cb in range(cols // LANES):
            cs = slice(cb * LANES, (cb + 1) * LANES)
            s = s_ref[:, cs]
            m_old = m_ref[:, cs]
            m_new = jnp.maximum(m_old, jnp.max(s, axis=0, keepdims=True))
            alpha = jnp.exp2(m_old - m_new)
            p = jnp.exp2(s - m_new)
            m_ref[:, cs] = m_new
            l_ref[:, cs] = alpha * l_ref[:, cs] + jnp.sum(p, axis=0, keepdims=True)
            p_ref[:, cs] = p.astype(BF16)
            alphas.append(alpha)
        alpha_row = jnp.concatenate(alphas, axis=-1)
        acc_ref[:, 0:cols] = acc_ref[:, 0:cols] * alpha_row + _dot(vt_ref[kb], p_ref[:, 0:cols])

    scores(sa_ref, 0)
    if nkb > 1:
        def body(i, carry):
            scores(sb_ref, 2 * i + 1)
            softmax_block(sa_ref, 2 * i)
            scores(sa_ref, 2 * i + 2)
            softmax_block(sb_ref, 2 * i + 1)
            return carry

        lax.fori_loop(0, nkb // 2 - 1, body, 0)
        scores(sb_ref, nkb - 1)
        softmax_block(sa_ref, nkb - 2)
        softmax_block(sb_ref, nkb - 1)
    else:
        softmax_block(sa_ref, 0)
    out_t = acc_ref[:, 0:cols] * (1.0 / l_ref[...])
    for g in range(B_GROUPS):
        o_ref[:, g * HEAD:(g + 1) * HEAD] = out_t[:, g * tq:(g + 1) * tq].T.astype(o_ref.dtype)


def _flash_tiles(seq_len):
    tq = _tile(seq_len, 256)
    tk = _tile(seq_len, 512)
    assert seq_len // tk == 1 or (seq_len // tk) % 2 == 0
    return tq, tk


def _flash(q_t, k_r, v_t, km, vmt, n_seq, seq_len):
    tq, tk = _flash_tiles(seq_len)
    nkb = seq_len // tk
    ntq = seq_len // tq
    cols = B_GROUPS * tq
    gw = B_GROUPS * HEAD
    k3 = k_r.reshape(n_seq, seq_len, B_KV_DIM)
    out = pl.pallas_call(
        functools.partial(_flash_kernel, tq=tq, tk=tk, nkb=nkb),
        grid=(n_seq, B_KV_HEADS, ntq),
        in_specs=[
            pl.BlockSpec((None, gw, tq), lambda b, j, i: (b * ntq + i, j, 0)),
            pl.BlockSpec((None, seq_len, HEAD), lambda b, j, i: (b, 0, j)),
            pl.BlockSpec((None, nkb, HEAD, tk), lambda b, j, i: (j, b, 0, 0)),
            pl.BlockSpec((LANES, HEAD), lambda b, j, i: (0, j)),
            pl.BlockSpec((None, HEAD, LANES), lambda b, j, i: (j, 0, 0)),
        ],
        out_specs=pl.BlockSpec((None, tq, gw), lambda b, j, i: (b, i, j)),
        out_shape=jax.ShapeDtypeStruct((n_seq, seq_len, B_Q_DIM), BF16),
        scratch_shapes=[
            pltpu.VMEM((HEAD, cols), BF16),
            pltpu.VMEM((tk, cols + LANES), F32),
            pltpu.VMEM((tk, cols + LANES), F32),
            pltpu.VMEM((tk, cols + LANES), BF16),
            pltpu.VMEM((1, cols), F32),
            pltpu.VMEM((1, cols), F32),
            pltpu.VMEM((HEAD, cols + LANES), F32),
        ],
        compiler_params=_cparams(("parallel", "parallel", "arbitrary")),
        name="flash_attn",
    )(q_t, k3, v_t, km, vmt)
    return out.reshape(n_seq * seq_len, B_Q_DIM)


CONV_CB = 512
HALO = 8


def _conv_kernel(x_ref, prev_ref, next_ref, ab_ref, cw_ref, alog_ref, dtb_ref,
                 q_ref, k_ref, v_ref, gc_ref, gt_ref, xs_ref, *, tm, pad_rows):
    row = lax.broadcasted_iota(jnp.int32, (tm, 1), 0)
    valid = (row % CHUNK) >= pad_rows
    for cb in range(CONV_CH // CONV_CB):
        cs = slice(cb * CONV_CB, (cb + 1) * CONV_CB)
        xs_ref[0:HALO, :] = prev_ref[0, :, cs].astype(F32)
        xs_ref[HALO:HALO + tm, :] = x_ref[:, cs].astype(F32)
        xs_ref[HALO + tm:, :] = next_ref[0, :, cs].astype(F32)
        y = jnp.zeros((tm, CONV_CB), F32)
        for w in range(CONV_W):
            y = y + xs_ref[pl.ds(HALO - CONV_W // 2 + w, tm), :] * cw_ref[w:w + 1, cs]
        y = y * (1.0 / (1.0 + jnp.exp(-y)))
        if pad_rows:
            y = jnp.where(valid, y, 0.0)
        for hh in range(CONV_CB // HEAD):
            col = cb * CONV_CB + hh * HEAD
            yh = y[:, hh * HEAD:(hh + 1) * HEAD]
            if col < 2 * A_KEY_DIM:
                yh = yh * lax.rsqrt(jnp.sum(yh * yh, axis=-1, keepdims=True) + EPS)
            if col < A_KEY_DIM:
                q_ref[col // HEAD] = (yh * (HEAD ** -0.5)).astype(q_ref.dtype)
            elif col < 2 * A_KEY_DIM:
                k_ref[(col - A_KEY_DIM) // HEAD] = yh.astype(k_ref.dtype)
            else:
                v_ref[(col - 2 * A_KEY_DIM) // HEAD] = yh.astype(v_ref.dtype)

    ab = ab_ref[...]
    lane = lax.broadcasted_iota(jnp.int32, (tm, LANES), 1)
    is_gate = lane < GATE_LANES
    is_a = is_gate & ((lane % 8) < 4)
    z = ab + dtb_ref[...]
    softplus = jnp.maximum(z, 0.0) + jnp.log1p(jnp.exp(-jnp.abs(z)))
    g = -jnp.exp(alog_ref[...]) * softplus
    beta = 1.0 / (1.0 + jnp.exp(-ab))
    g = jnp.where(valid & is_a, g, 0.0)
    beta = jnp.where(valid & is_gate & (~is_a), beta, 0.0)

    r = lax.broadcasted_iota(jnp.int32, (CHUNK, CHUNK), 0)
    c = lax.broadcasted_iota(jnp.int32, (CHUNK, CHUNK), 1)
    lower = (c <= r).astype(F32)
    upper = (c >= r).astype(F32)
    ones = jnp.ones((CHUNK, CHUNK), F32)
    lane_c = lax.broadcasted_iota(jnp.int32, (CHUNK, LANES), 1)
    fwd_lane = (lane_c % 4) < 2
    a_lane = (lane_c % 8) < 4
    for ck in range(tm // CHUNK):
        rs = slice(ck * CHUNK, (ck + 1) * CHUNK)
        gk = g[rs]
        gcum = jnp.where(fwd_lane, _dot_f32(lower, gk), _dot_f32(upper, gk))
        gtot = _dot_f32(ones, gk)
        e = jnp.exp(gcum)
        dec = jnp.exp(gtot - gcum)
        eg = jnp.exp(gtot)
        bk = beta[rs]
        t0 = jnp.where(a_lane, gcum, bk)
        t1 = jnp.where(a_lane, e, bk * pltpu.roll(e, 4, 1))
        t2 = jnp.where(a_lane, dec, pltpu.roll(eg, 4, 1))
        gc_ref[rs, 0:LANES] = t0
        gc_ref[rs, LANES:2 * LANES] = t1
        gc_ref[rs, 2 * LANES:3 * LANES] = t2
        gt_ref[0, 0:LANES, rs] = t0.T
        gt_ref[0, LANES:2 * LANES, rs] = t1.T
        gt_ref[0, 2 * LANES:3 * LANES, rs] = t2.T


def _conv(x, col_block, prev, nxt, ab, cw, alog, dtb, tm, pad_rows):
    t = x.shape[0]
    nt = t // tm
    return pl.pallas_call(
        functools.partial(_conv_kernel, tm=tm, pad_rows=pad_rows),
        grid=(nt,),
        in_specs=[
            pl.BlockSpec((tm, CONV_CH), lambda i: (i, col_block)),
            pl.BlockSpec((1, HALO, CONV_CH), lambda i: (i, 0, 0)),
            pl.BlockSpec((1, HALO, CONV_CH), lambda i: (i, 0, 0)),
            pl.BlockSpec((tm, LANES), lambda i: (i, 0)),
            pl.BlockSpec((HALO, CONV_CH), lambda i: (0, 0)),
            pl.BlockSpec((1, LANES), lambda i: (0, 0)),
            pl.BlockSpec((1, LANES), lambda i: (0, 0)),
        ],
        out_specs=[
            pl.BlockSpec((A_QK_HEADS, tm, HEAD), lambda i: (0, i, 0)),
            pl.BlockSpec((A_QK_HEADS, tm, HEAD), lambda i: (0, i, 0)),
            pl.BlockSpec((A_V_HEADS, tm, HEAD), lambda i: (0, i, 0)),
            pl.BlockSpec((tm, 3 * LANES), lambda i: (i, 0)),
            pl.BlockSpec((1, 3 * LANES, tm), lambda i: (i, 0, 0)),
        ],
        out_shape=[
            jax.ShapeDtypeStruct((A_QK_HEADS, t, HEAD), BF16),
            jax.ShapeDtypeStruct((A_QK_HEADS, t, HEAD), BF16),
            jax.ShapeDtypeStruct((A_V_HEADS, t, HEAD), BF16),
            jax.ShapeDtypeStruct((t, 3 * LANES), F32),
            jax.ShapeDtypeStruct((nt, 3 * LANES, tm), F32),
        ],
        scratch_shapes=[pltpu.VMEM((tm + 2 * HALO, CONV_CB), F32)],
        compiler_params=_cparams(("parallel",)),
        name="conv_gates",
    )(x, prev, nxt, ab, cw, alog, dtb)


def _pre_kernel(q_ref, k_ref, gc_ref, gt_ref, tutw_ref, qk_ref, *, nck):
    hq = pl.program_id(1)
    r = lax.broadcasted_iota(jnp.int32, (CHUNK, LANES), 0)
    lane = lax.broadcasted_iota(jnp.int32, (CHUNK, LANES), 1)
    c = lane & (CHUNK - 1)
    left = lane < CHUNK

    def same(k):
        return (r >> k) == (c >> k)

    eye = jnp.where(r == c, 1.0, 0.0)
    same1 = same(1)
    levels = []
    k = 1
    while (1 << k) < CHUNK:
        levels.append(same(k + 1) & (~same(k)))
        k += 1

    def blockdiag(m):
        mb = m.astype(BF16)
        zero = jnp.zeros_like(mb)
        return jnp.concatenate([jnp.where(left, mb, zero), jnp.where(left, zero, mb)], axis=0)

    def pair_cols(x, rs, i):
        return jnp.where(left, x[rs, i:i + 1], x[rs, i + 1:i + 2])

    def pair_rows(x, i):
        return jnp.concatenate([x[i:i + 1], x[i + 1:i + 2]], axis=-1)

    g0 = pltpu.roll(gc_ref[:, 0:LANES], lax.rem(LANES - 8 * hq, LANES), 1)
    row0 = pl.multiple_of(8 * hq, 8)
    chains = []
    for ck in range(nck):
        rs = slice(ck * CHUNK, (ck + 1) * CHUNK)
        qb = q_ref[rs, :]
        kb = k_ref[rs, :]
        k2 = jnp.concatenate([kb, kb], axis=0)
        kk = _dot_nt(kb, k2)
        qk = _dot_nt(qb, k2)
        rows0 = gt_ref[pl.ds(row0, 8), rs]
        rows1 = gt_ref[pl.ds(LANES + row0, 8), rs]
        for d in range(2):
            incl = (c <= r) if d == 0 else (c >= r)
            strict = (c < r) if d == 0 else (c > r)
            gl = 2 * d
            gdiff = pair_cols(g0, rs, gl) - pair_rows(rows0, gl)
            decay = jnp.where(incl, jnp.exp(jnp.where(incl, gdiff, 0.0)), 0.0)
            qk_ref[d, rs, :] = (qk * decay).astype(qk_ref.dtype)
            a = jnp.where(strict, kk * decay, 0.0) * pair_cols(g0, rs, 4 + gl)
            chains.append(dict(a=a, t=eye - jnp.where(same1, a, 0.0), d=d, rs=rs,
                               beta=pair_rows(rows0, 4 + gl), be=pair_rows(rows1, 4 + gl)))
    for mask in levels:
        xs = [_dot(jnp.where(mask, ch["a"], 0.0).astype(BF16), blockdiag(ch["t"])) for ch in chains]
        ys = [_dot(ch["t"].astype(BF16), blockdiag(x)) for ch, x in zip(chains, xs)]
        for ch, y in zip(chains, ys):
            ch["t"] = ch["t"] - y
    for ch in chains:
        tu = ch["t"] * ch["beta"]
        tw = ch["t"] * ch["be"]
        d, rs = ch["d"], ch["rs"]
        tutw_ref[d, 0, rs, :] = jnp.where(left, tu, pltpu.roll(tw, CHUNK, 1)).astype(tutw_ref.dtype)
        tutw_ref[d, 1, rs, :] = jnp.where(left, pltpu.roll(tu, CHUNK, 1), tw).astype(tutw_ref.dtype)


def _pre(qh, kh, gates_c, gates_t):
    t = qh.shape[1]
    ct = gates_t.shape[2]
    return pl.pallas_call(
        functools.partial(_pre_kernel, nck=ct // CHUNK),
        grid=(t // ct, A_QK_HEADS),
        in_specs=[
            pl.BlockSpec((None, ct, HEAD), lambda i, h: (h, i, 0)),
            pl.BlockSpec((None, ct, HEAD), lambda i, h: (h, i, 0)),
            pl.BlockSpec((ct, 3 * LANES), lambda i, h: (i, 0)),
            pl.BlockSpec((None, 3 * LANES, ct), lambda i, h: (i, 0, 0)),
        ],
        out_specs=[
            pl.BlockSpec((2, 2, ct, HEAD), lambda i, h: (0, h, i, 0)),
            pl.BlockSpec((2, None, ct, HEAD), lambda i, h: (0, h, i, 0)),
        ],
        out_shape=[
            jax.ShapeDtypeStruct((2, A_V_HEADS, t, HEAD), BF16),
            jax.ShapeDtypeStruct((2, A_QK_HEADS, t, HEAD), BF16),
        ],
        compiler_params=_cparams(("parallel", "arbitrary")),
        name="chunk_pre",
    )(qh, kh, gates_c, gates_t)


def _rec_kernel(q_ref, k_ref, v_ref, tutw_ref, qk_ref, g_ref, s0_ref, o_ref, sfin_ref, s_ref, *, nc):
    d = pl.program_id(1)
    cidx = pl.program_id(2)

    @pl.when(cidx == 0)
    def _():
        s_ref[...] = s0_ref[...]

    def dir_lanes(x):
        return jnp.where(d == 0, x, pltpu.roll(x, LANES - 2, 1))

    g1 = dir_lanes(g_ref[:, LANES:2 * LANES])
    g2 = dir_lanes(g_ref[:, 2 * LANES:3 * LANES])
    heads = range(A_V_HEADS)
    qk_lhs = [jnp.concatenate([q_ref[hq], k_ref[hq]], axis=0) for hq in range(A_QK_HEADS)]
    k_t = [k_ref[hq].astype(F32).T.astype(BF16) for hq in range(A_QK_HEADS)]
    states = [s_ref[j] for j in heads]
    qs_ks = [_dot(qk_lhs[j // 2], states[j].astype(BF16)) for j in heads]
    v_new = [_dot(tutw_ref[j], jnp.concatenate([v_ref[j], (-qs_ks[j][CHUNK:]).astype(BF16)], axis=0))
             for j in heads]
    for j in heads:
        hq, jj = divmod(j, 2)
        gl = 8 * hq + jj
        qkm = qk_ref[hq][:, jj * CHUNK:(jj + 1) * CHUNK]
        o = g1[:, gl:gl + 1] * qs_ks[j][:CHUNK] + _dot(qkm, v_new[j].astype(BF16))
        o_ref[j] = o.astype(o_ref.dtype)
    for j in heads:
        hq, jj = divmod(j, 2)
        gl = 8 * hq + jj
        v_dec = (v_new[j] * g2[:, gl:gl + 1]).astype(BF16)
        s_ref[j] = states[j] * g2[0:1, 4 + gl:5 + gl] + _dot(k_t[hq], v_dec)

    @pl.when(cidx == nc - 1)
    def _():
        sfin_ref[...] = s_ref[...]


def _rec(qh, kh, vh, tutw, qkm, gates_c, s0, n_seq, n_dir):
    t = qh.shape[1]
    nc = t // n_seq // CHUNK

    def chunk(b, d, c):
        return b * nc + c + d * (nc - 1 - 2 * c)

    return pl.pallas_call(
        functools.partial(_rec_kernel, nc=nc),
        grid=(n_seq, n_dir, nc),
        in_specs=[
            pl.BlockSpec((A_QK_HEADS, CHUNK, HEAD), lambda b, d, c: (0, chunk(b, d, c), 0)),
            pl.BlockSpec((A_QK_HEADS, CHUNK, HEAD), lambda b, d, c: (0, chunk(b, d, c), 0)),
            pl.BlockSpec((A_V_HEADS, CHUNK, HEAD), lambda b, d, c: (0, chunk(b, d, c), 0)),
            pl.BlockSpec((None, A_V_HEADS, CHUNK, HEAD), lambda b, d, c: (d, 0, chunk(b, d, c), 0)),
            pl.BlockSpec((None, A_QK_HEADS, CHUNK, HEAD), lambda b, d, c: (d, 0, chunk(b, d, c), 0)),
            pl.BlockSpec((CHUNK, 3 * LANES), lambda b, d, c: (chunk(b, d, c), 0)),
            pl.BlockSpec((None, None, A_V_HEADS, HEAD, HEAD), lambda b, d, c: (b, d, 0, 0, 0)),
        ],
        out_specs=[
            pl.BlockSpec((None, A_V_HEADS, CHUNK, HEAD), lambda b, d, c: (d, 0, chunk(b, d, c), 0)),
            pl.BlockSpec((None, None, A_V_HEADS, HEAD, HEAD), lambda b, d, c: (b, d, 0, 0, 0)),
        ],
        out_shape=[
            jax.ShapeDtypeStruct((n_dir, A_V_HEADS, t, HEAD), F32),
            jax.ShapeDtypeStruct((n_seq, n_dir, A_V_HEADS, HEAD, HEAD), F32),
        ],
        scratch_shapes=[pltpu.VMEM((A_V_HEADS, HEAD, HEAD), F32)],
        compiler_params=_cparams(("parallel", "parallel", "arbitrary")),
        name="chunk_rec",
    )(qh, kh, vh, tutw, qkm, gates_c, s0)


def _post_kernel(o_ref, z_ref, attn_ref, ga_ref, gb_ref, x_ref, na_ref, wa_ref, wb_ref, wo_ref,
                 nf_ref, wr_ref, br_ref, h_ref, xn_ref, comb_ref):
    parts = []
    for j in range(A_V_HEADS):
        o = o_ref[0, j] + o_ref[1, j]
        on = o * lax.rsqrt(jnp.mean(o * o, axis=-1, keepdims=True) + EPS) * na_ref[...]
        z = z_ref[:, j * HEAD:(j + 1) * HEAD].astype(F32)
        parts.append((on * (z * (1.0 / (1.0 + jnp.exp(-z))))).astype(BF16))
    ya = _dot(jnp.concatenate(parts, axis=-1), wa_ref[...])
    yb = _dot(attn_ref[...], wb_ref[...])
    sig_a = 1.0 / (1.0 + jnp.exp(-ga_ref[...].astype(F32)))
    sig_b = 1.0 / (1.0 + jnp.exp(-gb_ref[...].astype(F32)))
    merged = sig_a * ya + sig_b * yb
    h = x_ref[...] + _dot(merged.astype(BF16), wo_ref[...])
    h_ref[...] = h
    xn = h * lax.rsqrt(jnp.mean(h * h, axis=-1, keepdims=True) + EPS) * nf_ref[...]
    xn_ref[...] = xn.astype(xn_ref.dtype)

    x_hi = xn.astype(BF16)
    x_lo = (xn - x_hi.astype(F32)).astype(BF16)
    logits = (_dot(x_hi, wr_ref[0]) + _dot(x_lo, wr_ref[0]) + _dot(x_hi, wr_ref[1])) + br_ref[...]
    lane = lax.broadcasted_iota(jnp.int32, logits.shape, 1)
    lane_f = lane.astype(F32)
    g_mask = (lane >= N_EXPERTS) & (lane < N_EXPERTS + N_GROUPS)
    gl = jnp.where(g_mask, logits, NEG_BIG)
    g_max = jnp.max(gl, axis=-1, keepdims=True)
    g_w = 1.0 / jnp.sum(jnp.exp(gl - g_max), axis=-1, keepdims=True)
    g_idx = jnp.min(jnp.where(gl == g_max, lane_f, 1e9), axis=-1, keepdims=True) - N_EXPERTS
    e_mask = (lane < N_EXPERTS) & ((lane // EXPERTS_PER_GROUP).astype(F32) == g_idx)
    el = jnp.where(e_mask, logits, NEG_BIG)
    m1 = jnp.max(el, axis=-1, keepdims=True)
    i1 = jnp.min(jnp.where(el == m1, lane_f, 1e9), axis=-1, keepdims=True)
    el2 = jnp.where(lane_f == i1, NEG_BIG, el)
    m2 = jnp.max(el2, axis=-1, keepdims=True)
    i2 = jnp.min(jnp.where(el2 == m2, lane_f, 1e9), axis=-1, keepdims=True)
    e2 = jnp.exp(m2 - m1)
    w1 = g_w / (1.0 + e2)
    w2 = g_w * e2 / (1.0 + e2)
    comb_ref[...] = jnp.where(lane_f == i1, w1, 0.0) + jnp.where(lane_f == i2, w2, 0.0)


def _post(o, proj, attn, x2, na, wa, wb, wo, nf, wr, br):
    t = x2.shape[0]
    tm = _tile(t, 256)
    const = lambda i: (0, 0)
    return pl.pallas_call(
        _post_kernel,
        grid=(t // tm,),
        in_specs=[
            pl.BlockSpec((2, A_V_HEADS, tm, HEAD), lambda i: (0, 0, i, 0)),
            pl.BlockSpec((tm, A_VAL_DIM), lambda i: (i, OFF_Z // A_VAL_DIM)),
            pl.BlockSpec((tm, B_Q_DIM), lambda i: (i, 0)),
            pl.BlockSpec((tm, D_MODEL), lambda i: (i, OFF_GA // D_MODEL)),
            pl.BlockSpec((tm, D_MODEL), lambda i: (i, OFF_GB // D_MODEL)),
            pl.BlockSpec((tm, D_MODEL), lambda i: (i, 0)),
            pl.BlockSpec((1, HEAD), const),
            pl.BlockSpec((A_VAL_DIM, D_MODEL), const),
            pl.BlockSpec((B_Q_DIM, D_MODEL), const),
            pl.BlockSpec((D_MODEL, D_MODEL), const),
            pl.BlockSpec((1, D_MODEL), const),
            pl.BlockSpec((2, D_MODEL, LANES), lambda i: (0, 0, 0)),
            pl.BlockSpec((1, LANES), const),
        ],
        out_specs=[
            pl.BlockSpec((tm, D_MODEL), lambda i: (i, 0)),
            pl.BlockSpec((tm, D_MODEL), lambda i: (i, 0)),
            pl.BlockSpec((tm, LANES), lambda i: (i, 0)),
        ],
        out_shape=[
            jax.ShapeDtypeStruct((t, D_MODEL), F32),
            jax.ShapeDtypeStruct((t, D_MODEL), BF16),
            jax.ShapeDtypeStruct((t, LANES), F32),
        ],
        compiler_params=_cparams(("parallel",)),
        name="post_mixer",
    )(o, proj, attn, proj, proj, x2, na, wa, wb, wo, nf, wr, br)


def _moe_kernel(xn_ref, comb_ref, h_ref, wgu_ref, wd_ref, nfin_ref, y_ref):
    e = pl.program_id(1)
    acc_ref = y_ref

    @pl.when(e == 0)
    def _():
        acc_ref[...] = h_ref[...]

    gu = _dot(xn_ref[...], wgu_ref[...])
    gate = gu[:, :D_FF_EXPERT]
    up = gu[:, D_FF_EXPERT:]
    lane = lax.broadcasted_iota(jnp.int32, comb_ref.shape, 1)
    w = jnp.sum(jnp.where(lane == e, comb_ref[...], 0.0), axis=-1, keepdims=True)
    hid = gate * (1.0 / (1.0 + jnp.exp(-gate))) * up
    acc_ref[...] += w * _dot(hid.astype(BF16), wd_ref[...])

    @pl.when(e == N_EXPERTS - 1)
    def _():
        a = acc_ref[...]
        y_ref[...] = a * lax.rsqrt(jnp.mean(a * a, axis=-1, keepdims=True) + EPS) * nfin_ref[...]


def _moe(xn, comb, h, wgu, wd, nfin):
    t = xn.shape[0]
    tm = _tile(t, 2048)
    return pl.pallas_call(
        _moe_kernel,
        grid=(t // tm, N_EXPERTS),
        in_specs=[
            pl.BlockSpec((tm, D_MODEL), lambda i, e: (i, 0)),
            pl.BlockSpec((tm, LANES), lambda i, e: (i, 0)),
            pl.BlockSpec((tm, D_MODEL), lambda i, e: (i, 0)),
            pl.BlockSpec((None, D_MODEL, 2 * D_FF_EXPERT), lambda i, e: (e, 0, 0)),
            pl.BlockSpec((None, D_FF_EXPERT, D_MODEL), lambda i, e: (e, 0, 0)),
            pl.BlockSpec((1, D_MODEL), lambda i, e: (0, 0)),
        ],
        out_specs=pl.BlockSpec((tm, D_MODEL), lambda i, e: (i, 0)),
        out_shape=jax.ShapeDtypeStruct((t, D_MODEL), F32),
        compiler_params=_cparams(("parallel", "arbitrary")),
        name="moe_experts",
    )(xn, comb, h, wgu, wd, nfin)


def _forward(x, p):
    n_seq, seq_len, _ = x.shape
    t = n_seq * seq_len
    x2 = x.reshape(t, D_MODEL)
    proj, ab = _inproj(x2, p["norm_mix"], p["w_main"], p["w_ab"])

    tq, tk = _flash_tiles(seq_len)
    q_t, k_r, v_t = _attn_prep(proj, p["cos"], p["sin"], p["norm_q_b"], p["norm_k_b"], seq_len, tq, tk)
    attn = _flash(q_t, k_r, v_t, p["k_meta"], p["v_meta_t"], n_seq, seq_len)

    zeros_halo = jnp.zeros((n_seq, HALO, CONV_CH), BF16)
    first_rows = proj.reshape(n_seq, seq_len, N_MAIN)[:, :HALO, :CONV_CH]
    meta_x = jnp.broadcast_to(p["meta_chunk_x"][None], (n_seq, CHUNK, CONV_CH)).reshape(n_seq * CHUNK, CONV_CH)
    meta_ab = jnp.broadcast_to(p["meta_chunk_ab"][None], (n_seq, CHUNK, LANES)).reshape(n_seq * CHUNK, LANES)
    conv_args = (p["conv_w"], p["a_log"], p["dt_bias"])
    mq, mk, mv, mgc, mgt = _conv(meta_x, 0, zeros_halo, first_rows, meta_ab, *conv_args, tm=CHUNK, pad_rows=PAD_A)
    m_tutw, m_qk = _pre(mq, mk, mgc, mgt)
    zero_state = jnp.zeros((n_seq, 1, A_V_HEADS, HEAD, HEAD), F32)
    _, s_meta = _rec(mq, mk, mv, m_tutw, m_qk, mgc, zero_state, n_seq, 1)

    tm = _tile(seq_len, 512)
    per_seq = seq_len // tm
    tiles = proj.reshape(n_seq, per_seq, tm, N_MAIN)
    tails = tiles[:, :, tm - HALO:, :CONV_CH]
    heads = tiles[:, :, :HALO, :CONV_CH]
    meta_tail = jnp.broadcast_to(p["meta_chunk_x"][None, None, CHUNK - HALO:], (n_seq, 1, HALO, CONV_CH))
    prev = jnp.concatenate([meta_tail, tails[:, :-1]], axis=1).reshape(n_seq * per_seq, HALO, CONV_CH)
    nxt = jnp.concatenate([heads[:, 1:], zeros_halo[:, None]], axis=1).reshape(n_seq * per_seq, HALO, CONV_CH)
    qh, kh, vh, gates_c, gates_t = _conv(proj, OFF_QKV // CONV_CH, prev, nxt, ab, *conv_args, tm=tm, pad_rows=0)
    tutw, qkm = _pre(qh, kh, gates_c, gates_t)
    s0 = jnp.concatenate([s_meta, zero_state], axis=1)
    o, _ = _rec(qh, kh, vh, tutw, qkm, gates_c, s0, n_seq, 2)

    h, xn, comb = _post(o, proj, attn, x2, p["norm_out_a"], p["w_branch_a"], p["w_branch_b"], p["w_out"],
                        p["norm_ffn"], p["w_route"], p["b_route"])
    y = _moe(xn, comb, h, p["w_gate_up"], p["w_down"], p["norm_final"])
    return y.reshape(n_seq, seq_len, D_MODEL)


def _prepare(meta_tokens, norm_mix, w_in, conv_w, a_log_fwd, a_log_bwd, dt_bias_fwd, dt_bias_bwd,
             norm_out_a, norm_q_b, norm_k_b, w_branch_a, w_branch_b, w_out, norm_ffn, w_group, b_group,
             w_router, b_router, w_gate_e, w_up_e, w_down_e, norm_final, max_len):
    w = w_in[0]
    bounds = np.cumsum([0, A_KEY_DIM, A_KEY_DIM, A_VAL_DIM, A_VAL_DIM, 4 * A_V_HEADS, B_Q_DIM, B_KV_DIM,
                        B_KV_DIM, D_MODEL, D_MODEL])
    qa, ka, va, za, abw, qb, kb, vb, ga, gb = [w[:, bounds[i]:bounds[i + 1]] for i in range(10)]
    p = {}
    p["w_main"] = jnp.concatenate([qa, ka, va, za, qb, ga, gb, kb, vb], axis=1).astype(BF16)
    src = _gate_source_columns()
    p["w_ab"] = jnp.pad(abw[:, src], ((0, 0), (0, LANES - GATE_LANES))).astype(BF16)
    p["norm_mix"] = norm_mix[0][None]
    p["conv_w"] = jnp.pad(conv_w[0], ((0, HALO - CONV_W), (0, 0)))
    pad_lanes = lambda v: jnp.pad(v, (0, LANES - v.shape[0]))[None]
    zeros_b = jnp.zeros((2 * A_V_HEADS,), F32)
    p["a_log"] = pad_lanes(jnp.concatenate([a_log_fwd[0], a_log_bwd[0], zeros_b])[src])
    p["dt_bias"] = pad_lanes(jnp.concatenate([dt_bias_fwd[0], dt_bias_bwd[0], zeros_b])[src])
    p["norm_out_a"] = norm_out_a[0][None]
    p["norm_q_b"] = norm_q_b[0][None]
    p["norm_k_b"] = norm_k_b[0][None]
    p["w_branch_a"] = w_branch_a[0].astype(BF16)
    p["w_branch_b"] = w_branch_b[0].astype(BF16)
    p["w_out"] = w_out[0].astype(BF16)
    p["norm_ffn"] = norm_ffn[0][None]
    w_route = jnp.pad(jnp.concatenate([w_router[0], w_group[0]], axis=1),
                      ((0, 0), (0, LANES - N_EXPERTS - N_GROUPS)))
    w_route_hi = w_route.astype(BF16)
    p["w_route"] = jnp.stack([w_route_hi, (w_route - w_route_hi.astype(F32)).astype(BF16)])
    p["b_route"] = pad_lanes(jnp.concatenate([b_router[0], b_group[0]]))
    p["w_gate_up"] = jnp.concatenate([w_gate_e[0], w_up_e[0]], axis=-1).astype(BF16)
    p["w_down"] = w_down_e[0].astype(BF16)
    p["norm_final"] = norm_final[None]

    pos = jnp.arange(max_len, dtype=jnp.int32)
    p["cos"], p["sin"] = _rope_tables((pos // GRID_W).astype(F32), (pos % GRID_W).astype(F32))
    cos_m, sin_m = _rope_tables(jnp.full((N_META,), -1.0, F32), jnp.arange(N_META, dtype=F32))

    mproj, mab = _inproj(meta_tokens, p["norm_mix"], p["w_main"], p["w_ab"])
    k_meta = _meta_keys(mproj, cos_m, sin_m, p["norm_k_b"])
    p["k_meta"] = jnp.pad(k_meta, ((0, LANES - N_META), (0, 0)))
    v_meta = jnp.pad(mproj[:, OFF_VB:OFF_VB + B_KV_DIM], ((0, LANES - N_META), (0, 0)))
    p["v_meta_t"] = v_meta.reshape(LANES, B_KV_HEADS, HEAD).transpose(1, 2, 0)
    p["meta_chunk_x"] = jnp.pad(mproj[:, :CONV_CH], ((PAD_A, 0), (0, 0)))
    p["meta_chunk_ab"] = jnp.pad(mab, ((PAD_A, 0), (0, 0)))
    return p


def kernel(x_prompt, x_sample, meta_tokens, norm_mix, w_in, conv_w, a_log_fwd, a_log_bwd, dt_bias_fwd,
           dt_bias_bwd, norm_out_a, norm_q_b, norm_k_b, w_branch_a, w_branch_b, w_out, norm_ffn, w_group,
           b_group, w_router, b_router, w_gate_e, w_up_e, w_down_e, norm_final):
    max_len = max(x_prompt.shape[1], x_sample.shape[1])
    p = _prepare(meta_tokens, norm_mix, w_in, conv_w, a_log_fwd, a_log_bwd, dt_bias_fwd, dt_bias_bwd,
                 norm_out_a, norm_q_b, norm_k_b, w_branch_a, w_branch_b, w_out, norm_ffn, w_group, b_group,
                 w_router, b_router, w_gate_e, w_up_e, w_down_e, norm_final, max_len)
    return (_forward(x_prompt, p), _forward(x_sample, p))
```

```python
import functools
import math

import numpy as np
import jax
import jax.numpy as jnp
from jax import lax
from jax.experimental import pallas as pl
from jax.experimental.pallas import tpu as pltpu

F32 = jnp.float32
BF16 = jnp.bfloat16

D_MODEL = 1024
N_META = 16
GRID_W = 64
EPS = 1e-6
HEAD = 128
A_QK_HEADS = 8
A_V_HEADS = 16
A_KEY_DIM = A_QK_HEADS * HEAD
A_VAL_DIM = A_V_HEADS * HEAD
CONV_W = 5
CONV_CH = 2 * A_KEY_DIM + A_VAL_DIM
CHUNK = 64
PAD_A = CHUNK - N_META
B_Q_HEADS = 8
B_KV_HEADS = 2
B_GROUPS = B_Q_HEADS // B_KV_HEADS
B_Q_DIM = B_Q_HEADS * HEAD
B_KV_DIM = B_KV_HEADS * HEAD
ROPE_THETA = 10000.0
N_GROUPS = 4
EXPERTS_PER_GROUP = 8
N_EXPERTS = N_GROUPS * EXPERTS_PER_GROUP
D_FF_EXPERT = 256

OFF_QKV = 0
OFF_Z = 4096
OFF_QB = 6144
OFF_GA = 7168
OFF_GB = 8192
OFF_KB = 9216
OFF_VB = 9472
N_MAIN = 9728
LANES = 128
GATE_LANES = 4 * A_V_HEADS
NEG_BIG = -1e30


def _gate_source_columns():
    src = []
    for lane in range(GATE_LANES):
        hq, kind, d, jj = lane // 8, (lane % 8) // 4, (lane % 4) // 2, lane % 2
        src.append(2 * A_V_HEADS * kind + A_V_HEADS * d + 2 * hq + jj)
    return np.asarray(src)
VMEM_LIMIT = 56 * 1024 * 1024


def _cparams(sem):
    return pltpu.CompilerParams(dimension_semantics=sem, vmem_limit_bytes=VMEM_LIMIT)


def _dot(a, b):
    return jnp.dot(a, b, preferred_element_type=F32)


def _dot_nt(a, b):
    return lax.dot_general(a, b, (((1,), (1,)), ((), ())), preferred_element_type=F32)


def _dot_tn(a, b):
    return lax.dot_general(a, b, (((0,), (0,)), ((), ())), preferred_element_type=F32)


def _dot_f32(a, b):
    return jnp.dot(a, b, preferred_element_type=F32, precision=lax.Precision.HIGHEST)


def _tile(n, pref):
    t = min(n, pref)
    assert n % t == 0, (n, t)
    return t


def _inproj_kernel(x_ref, g_ref, w_ref, wab_ref, o_ref, oab_ref, xn_ref):
    @pl.when(pl.program_id(1) == 0)
    def _():
        x = x_ref[...]
        xn = x * lax.rsqrt(jnp.mean(x * x, axis=-1, keepdims=True) + EPS) * g_ref[...]
        xb = xn.astype(BF16)
        xn_ref[...] = xb
        oab_ref[...] = _dot(xb, wab_ref[...])

    o_ref[...] = _dot(xn_ref[...], w_ref[...]).astype(o_ref.dtype)


def _inproj(x2, gain, w_main, w_ab):
    t = x2.shape[0]
    tm = _tile(t, 1024)
    tn = N_MAIN // 4
    return pl.pallas_call(
        _inproj_kernel,
        grid=(t // tm, N_MAIN // tn),
        in_specs=[
            pl.BlockSpec((tm, D_MODEL), lambda i, j: (i, 0)),
            pl.BlockSpec((1, D_MODEL), lambda i, j: (0, 0)),
            pl.BlockSpec((D_MODEL, tn), lambda i, j: (0, j)),
            pl.BlockSpec((D_MODEL, LANES), lambda i, j: (0, 0)),
        ],
        out_specs=[
            pl.BlockSpec((tm, tn), lambda i, j: (i, j)),
            pl.BlockSpec((tm, LANES), lambda i, j: (i, 0)),
        ],
        out_shape=[jax.ShapeDtypeStruct((t, N_MAIN), BF16), jax.ShapeDtypeStruct((t, LANES), F32)],
        scratch_shapes=[pltpu.VMEM((tm, D_MODEL), BF16)],
        compiler_params=_cparams(("parallel", "arbitrary")),
        name="in_proj",
    )(x2, gain, w_main, w_ab)


def _rope_tables(row, col):
    half = HEAD // 2
    inv_freq = ROPE_THETA ** (-jnp.arange(0, half, 2, dtype=F32) / half)
    ang_r = row[:, None] * inv_freq
    ang_c = col[:, None] * inv_freq
    cos = jnp.concatenate([jnp.cos(ang_r)] * 2 + [jnp.cos(ang_c)] * 2, axis=-1)
    sin = jnp.concatenate([-jnp.sin(ang_r), jnp.sin(ang_r), -jnp.sin(ang_c), jnp.sin(ang_c)], axis=-1)
    return cos, sin


def _norm_rope(x, gain, cos, sin, scale):
    lane = lax.broadcasted_iota(jnp.int32, cos.shape, 1)
    low = (lane % (HEAD // 2)) < (HEAD // 4)
    x = x.astype(F32)
    xn = x * lax.rsqrt(jnp.mean(x * x, axis=-1, keepdims=True) + EPS) * gain
    partner = jnp.where(low, pltpu.roll(xn, HEAD - HEAD // 4, 1), pltpu.roll(xn, HEAD // 4, 1))
    return (xn * cos + partner * sin) * scale


def _attn_prep_kernel(q_ref, k_ref, v_ref, cos_ref, sin_ref, gq_ref, gk_ref, qo_ref, ko_ref, vo_ref, *, tq):
    cos = cos_ref[...]
    sin = sin_ref[...]
    tm = cos.shape[0]
    for h in range(B_Q_HEADS):
        sl = slice(h * HEAD, (h + 1) * HEAD)
        q_t = _norm_rope(q_ref[:, sl], gq_ref[...], cos, sin, HEAD ** -0.5 * math.log2(math.e)).T
        for n in range(tm // tq):
            qo_ref[n, sl, :] = q_t[:, n * tq:(n + 1) * tq].astype(qo_ref.dtype)
    for h in range(B_KV_HEADS):
        sl = slice(h * HEAD, (h + 1) * HEAD)
        ko_ref[:, sl] = _norm_rope(k_ref[:, sl], gk_ref[...], cos, sin, 1.0).astype(ko_ref.dtype)
        vo_ref[h] = v_ref[:, sl].astype(F32).T.astype(vo_ref.dtype)


def _attn_prep(proj, cos, sin, gq, gk, seq_len, tq, tk):
    t = proj.shape[0]
    tm = tk
    per_seq = seq_len // tm
    return pl.pallas_call(
        functools.partial(_attn_prep_kernel, tq=tq),
        grid=(t // tm,),
        in_specs=[
            pl.BlockSpec((tm, B_Q_DIM), lambda i: (i, OFF_QB // B_Q_DIM)),
            pl.BlockSpec((tm, B_KV_DIM), lambda i: (i, OFF_KB // B_KV_DIM)),
            pl.BlockSpec((tm, B_KV_DIM), lambda i: (i, OFF_VB // B_KV_DIM)),
            pl.BlockSpec((tm, HEAD), lambda i: (i % per_seq, 0)),
            pl.BlockSpec((tm, HEAD), lambda i: (i % per_seq, 0)),
            pl.BlockSpec((1, HEAD), lambda i: (0, 0)),
            pl.BlockSpec((1, HEAD), lambda i: (0, 0)),
        ],
        out_specs=[
            pl.BlockSpec((tm // tq, B_Q_DIM, tq), lambda i: (i, 0, 0)),
            pl.BlockSpec((tm, B_KV_DIM), lambda i: (i, 0)),
            pl.BlockSpec((B_KV_HEADS, None, HEAD, tm), lambda i: (0, i, 0, 0)),
        ],
        out_shape=[
            jax.ShapeDtypeStruct((t // tq, B_Q_DIM, tq), BF16),
            jax.ShapeDtypeStruct((t, B_KV_DIM), BF16),
            jax.ShapeDtypeStruct((B_KV_HEADS, t // tm, HEAD, tm), BF16),
        ],
        compiler_params=_cparams(("parallel",)),
        name="attn_prep",
    )(proj, proj, proj, cos, sin, gq, gk)


def _meta_key_kernel(k_ref, cos_ref, sin_ref, gk_ref, ko_ref):
    for h in range(B_KV_HEADS):
        sl = slice(h * HEAD, (h + 1) * HEAD)
        ko_ref[:, sl] = _norm_rope(k_ref[:, sl], gk_ref[...], cos_ref[...], sin_ref[...], 1.0).astype(ko_ref.dtype)


def _meta_keys(mproj, cos, sin, gk):
    return pl.pallas_call(
        _meta_key_kernel,
        grid=(1,),
        in_specs=[
            pl.BlockSpec((N_META, B_KV_DIM), lambda i: (0, OFF_KB // B_KV_DIM)),
            pl.BlockSpec((N_META, HEAD), lambda i: (0, 0)),
            pl.BlockSpec((N_META, HEAD), lambda i: (0, 0)),
            pl.BlockSpec((1, HEAD), lambda i: (0, 0)),
        ],
        out_specs=pl.BlockSpec((N_META, B_KV_DIM), lambda i: (0, 0)),
        out_shape=jax.ShapeDtypeStruct((N_META, B_KV_DIM), BF16),
        name="meta_keys",
    )(mproj, cos, sin, gk)


ONES_ROWS = 16


def _flash_kernel(qt_ref, k_ref, vt_ref, km_ref, vmt_ref, o_ref,
                  qa_ref, sa_ref, sb_ref, m_ref, acc_ref, *, tq, tk, nkb):
    cols = B_GROUPS * tq
    for g in range(B_GROUPS):
        qa_ref[:, g * tq:(g + 1) * tq] = qt_ref[g * HEAD:(g + 1) * HEAD, :]

    def scores(s_ref, kb):
        start = pl.multiple_of(kb * tk, tk)
        s_ref[:, 0:cols] = _dot(k_ref[pl.ds(start, tk), :], qa_ref[...])

    scores(sa_ref, 0)

    def with_ones(v_t):
        return jnp.concatenate([v_t, jnp.ones((ONES_ROWS, v_t.shape[1]), BF16)], axis=0)

    s = _dot(km_ref[...], qa_ref[...])
    key = lax.broadcasted_iota(jnp.int32, s.shape, 0)
    s = jnp.where(key < N_META, s, NEG_BIG)
    m = jnp.max(s, axis=0, keepdims=True)
    m_ref[...] = m
    acc_ref[:, 0:cols] = _dot(with_ones(vmt_ref[...]), jnp.exp2((s - m).astype(BF16)))

    def softmax_block(s_ref, kb):
        v_t = with_ones(vt_ref[kb])
        for pair in range(cols // (2 * LANES)):
            alphas, probs = [], []
            for cb in (2 * pair, 2 * pair + 1):
                cs = slice(cb * LANES, (cb + 1) * LANES)
                s = s_ref[:, cs]
                m_old = m_ref[:, cs]
                m_new = jnp.maximum(m_old, jnp.max(s, axis=0, keepdims=True))
                alphas.append(jnp.exp2(m_old - m_new))
                m_ref[:, cs] = m_new
                probs.append(jnp.exp2((s - m_new).astype(BF16)))
            ps = slice(2 * pair * LANES, (2 * pair + 2) * LANES)
            acc_ref[:, ps] = (acc_ref[:, ps] * jnp.concatenate(alphas, axis=-1)
                              + _dot(v_t, jnp.concatenate(probs, axis=-1)))

    if nkb > 1:
        def body(i, carry):
            scores(sb_ref, 2 * i + 1)
            softmax_block(sa_ref, 2 * i)
            scores(sa_ref, 2 * i + 2)
            softmax_block(sb_ref, 2 * i + 1)
            return carry

        lax.fori_loop(0, nkb // 2 - 1, body, 0)
        scores(sb_ref, nkb - 1)
        softmax_block(sa_ref, nkb - 2)
        softmax_block(sb_ref, nkb - 1)
    else:
        softmax_block(sa_ref, 0)
    out_t = acc_ref[0:HEAD, 0:cols] * (1.0 / acc_ref[HEAD:HEAD + 1, 0:cols])
    for g in range(B_GROUPS):
        o_ref[:, g * HEAD:(g + 1) * HEAD] = out_t[:, g * tq:(g + 1) * tq].T.astype(o_ref.dtype)


def _flash_tiles(seq_len):
    tq = _tile(seq_len, 256)
    tk = _tile(seq_len, 512)
    assert seq_len // tk == 1 or (seq_len // tk) % 2 == 0
    return tq, tk


def _flash(q_t, k_r, v_t, km, vmt, n_seq, seq_len):
    tq, tk = _flash_tiles(seq_len)
    nkb = seq_len // tk
    ntq = seq_len // tq
    cols = B_GROUPS * tq
    gw = B_GROUPS * HEAD
    k3 = k_r.reshape(n_seq, seq_len, B_KV_DIM)
    out = pl.pallas_call(
        functools.partial(_flash_kernel, tq=tq, tk=tk, nkb=nkb),
        grid=(n_seq, B_KV_HEADS, ntq),
        in_specs=[
            pl.BlockSpec((None, gw, tq), lambda b, j, i: (b * ntq + i, j, 0)),
            pl.BlockSpec((None, seq_len, HEAD), lambda b, j, i: (b, 0, j)),
            pl.BlockSpec((None, nkb, HEAD, tk), lambda b, j, i: (j, b, 0, 0)),
            pl.BlockSpec((LANES, HEAD), lambda b, j, i: (0, j)),
            pl.BlockSpec((None, HEAD, LANES), lambda b, j, i: (j, 0, 0)),
        ],
        out_specs=pl.BlockSpec((None, tq, gw), lambda b, j, i: (b, i, j)),
        out_shape=jax.ShapeDtypeStruct((n_seq, seq_len, B_Q_DIM), BF16),
        scratch_shapes=[
            pltpu.VMEM((HEAD, cols), BF16),
            pltpu.VMEM((tk, cols + LANES), F32),
            pltpu.VMEM((tk, cols + LANES), F32),
            pltpu.VMEM((1, cols), F32),
            pltpu.VMEM((HEAD + ONES_ROWS, cols + LANES), F32),
        ],
        compiler_params=_cparams(("parallel", "parallel", "arbitrary")),
        name="flash_attn",
    )(q_t, k3, v_t, km, vmt)
    return out.reshape(n_seq * seq_len, B_Q_DIM)


CONV_CB = 512
HALO = 8


def _conv_kernel(x_ref, prev_ref, next_ref, ab_ref, cw_ref, alog_ref, dtb_ref,
                 q_ref, k_ref, v_ref, gc_ref, gt_ref, xs_ref, *, tm, pad_rows):
    row = lax.broadcasted_iota(jnp.int32, (tm, 1), 0)
    valid = (row % CHUNK) >= pad_rows
    for cb in range(CONV_CH // CONV_CB):
        cs = slice(cb * CONV_CB, (cb + 1) * CONV_CB)
        xs_ref[0:HALO, :] = prev_ref[0, :, cs].astype(F32)
        xs_ref[HALO:HALO + tm, :] = x_ref[:, cs].astype(F32)
        xs_ref[HALO + tm:, :] = next_ref[0, :, cs].astype(F32)
        y = jnp.zeros((tm, CONV_CB), F32)
        for w in range(CONV_W):
            y = y + xs_ref[pl.ds(HALO - CONV_W // 2 + w, tm), :] * cw_ref[w:w + 1, cs]
        y = y * (1.0 / (1.0 + jnp.exp(-y)))
        if pad_rows:
            y = jnp.where(valid, y, 0.0)
        for hh in range(CONV_CB // HEAD):
            col = cb * CONV_CB + hh * HEAD
            yh = y[:, hh * HEAD:(hh + 1) * HEAD]
            if col < 2 * A_KEY_DIM:
                yh = yh * lax.rsqrt(jnp.sum(yh * yh, axis=-1, keepdims=True) + EPS)
            if col < A_KEY_DIM:
                q_ref[col // HEAD] = (yh * (HEAD ** -0.5)).astype(q_ref.dtype)
            elif col < 2 * A_KEY_DIM:
                k_ref[(col - A_KEY_DIM) // HEAD] = yh.astype(k_ref.dtype)
            else:
                v_ref[(col - 2 * A_KEY_DIM) // HEAD] = yh.astype(v_ref.dtype)

    ab = ab_ref[...]
    lane = lax.broadcasted_iota(jnp.int32, (tm, LANES), 1)
    is_gate = lane < GATE_LANES
    is_a = is_gate & ((lane % 8) < 4)
    z = ab + dtb_ref[...]
    softplus = jnp.maximum(z, 0.0) + jnp.log1p(jnp.exp(-jnp.abs(z)))
    g = -jnp.exp(alog_ref[...]) * softplus
    beta = 1.0 / (1.0 + jnp.exp(-ab))
    g = jnp.where(valid & is_a, g, 0.0)
    beta = jnp.where(valid & is_gate & (~is_a), beta, 0.0)

    r = lax.broadcasted_iota(jnp.int32, (CHUNK, CHUNK), 0)
    c = lax.broadcasted_iota(jnp.int32, (CHUNK, CHUNK), 1)
    lower = (c <= r).astype(F32)
    upper = (c >= r).astype(F32)
    ones = jnp.ones((CHUNK, CHUNK), F32)
    lane_c = lax.broadcasted_iota(jnp.int32, (CHUNK, LANES), 1)
    fwd_lane = (lane_c % 4) < 2
    a_lane = (lane_c % 8) < 4
    for ck in range(tm // CHUNK):
        rs = slice(ck * CHUNK, (ck + 1) * CHUNK)
        gk = g[rs]
        gcum = jnp.where(fwd_lane, _dot_f32(lower, gk), _dot_f32(upper, gk))
        gtot = _dot_f32(ones, gk)
        e = jnp.exp(gcum)
        dec = jnp.exp(gtot - gcum)
        eg = jnp.exp(gtot)
        bk = beta[rs]
        t0 = jnp.where(a_lane, gcum, bk)
        t1 = jnp.where(a_lane, e, bk * pltpu.roll(e, 4, 1))
        t2 = jnp.where(a_lane, dec, pltpu.roll(eg, 4, 1))
        gc_ref[rs, 0:LANES] = t0
        gc_ref[rs, LANES:2 * LANES] = t1
        gc_ref[rs, 2 * LANES:3 * LANES] = t2
        gt_ref[0, 0:LANES, rs] = t0.T
        gt_ref[0, LANES:2 * LANES, rs] = t1.T
        gt_ref[0, 2 * LANES:3 * LANES, rs] = t2.T


def _conv(x, col_block, prev, nxt, ab, cw, alog, dtb, tm, pad_rows):
    t = x.shape[0]
    nt = t // tm
    return pl.pallas_call(
        functools.partial(_conv_kernel, tm=tm, pad_rows=pad_rows),
        grid=(nt,),
        in_specs=[
            pl.BlockSpec((tm, CONV_CH), lambda i: (i, col_block)),
            pl.BlockSpec((1, HALO, CONV_CH), lambda i: (i, 0, 0)),
            pl.BlockSpec((1, HALO, CONV_CH), lambda i: (i, 0, 0)),
            pl.BlockSpec((tm, LANES), lambda i: (i, 0)),
            pl.BlockSpec((HALO, CONV_CH), lambda i: (0, 0)),
            pl.BlockSpec((1, LANES), lambda i: (0, 0)),
            pl.BlockSpec((1, LANES), lambda i: (0, 0)),
        ],
        out_specs=[
            pl.BlockSpec((A_QK_HEADS, tm, HEAD), lambda i: (0, i, 0)),
            pl.BlockSpec((A_QK_HEADS, tm, HEAD), lambda i: (0, i, 0)),
            pl.BlockSpec((A_V_HEADS, tm, HEAD), lambda i: (0, i, 0)),
            pl.BlockSpec((tm, 3 * LANES), lambda i: (i, 0)),
            pl.BlockSpec((1, 3 * LANES, tm), lambda i: (i, 0, 0)),
        ],
        out_shape=[
            jax.ShapeDtypeStruct((A_QK_HEADS, t, HEAD), BF16),
            jax.ShapeDtypeStruct((A_QK_HEADS, t, HEAD), BF16),
            jax.ShapeDtypeStruct((A_V_HEADS, t, HEAD), BF16),
            jax.ShapeDtypeStruct((t, 3 * LANES), F32),
            jax.ShapeDtypeStruct((nt, 3 * LANES, tm), F32),
        ],
        scratch_shapes=[pltpu.VMEM((tm + 2 * HALO, CONV_CB), F32)],
        compiler_params=_cparams(("parallel",)),
        name="conv_gates",
    )(x, prev, nxt, ab, cw, alog, dtb)


PRE_HQ = 2


def _pre_kernel(q_ref, k_ref, gc_ref, gt_ref, tutw_ref, qk_ref, *, nck):
    r = lax.broadcasted_iota(jnp.int32, (CHUNK, LANES), 0)
    lane = lax.broadcasted_iota(jnp.int32, (CHUNK, LANES), 1)
    c = lane & (CHUNK - 1)
    left = lane < CHUNK

    def same(k):
        return (r >> k) == (c >> k)

    eye = jnp.where(r == c, 1.0, 0.0)
    same1 = same(1)
    levels = []
    k = 1
    while (1 << k) < CHUNK:
        levels.append(same(k + 1) & (~same(k)))
        k += 1

    def blockdiag(m):
        mb = m.astype(BF16)
        zero = jnp.zeros_like(mb)
        return jnp.concatenate([jnp.where(left, mb, zero), jnp.where(left, zero, mb)], axis=0)

    def pair_cols(x, rs, i):
        return jnp.where(left, x[rs, i:i + 1], x[rs, i + 1:i + 2])

    def pair_rows(x, i):
        return jnp.concatenate([x[i:i + 1], x[i + 1:i + 2]], axis=-1)

    chains = []
    for hh in range(PRE_HQ):
        hq = PRE_HQ * pl.program_id(1) + hh
        g0 = pltpu.roll(gc_ref[:, 0:LANES], lax.rem(LANES - 8 * hq, LANES), 1)
        row0 = pl.multiple_of(8 * hq, 8)
        for ck in range(nck):
            rs = slice(ck * CHUNK, (ck + 1) * CHUNK)
            qb = q_ref[hh, rs, :]
            kb = k_ref[hh, rs, :]
            k2 = jnp.concatenate([kb, kb], axis=0)
            kk = _dot_nt(kb, k2)
            qk = _dot_nt(qb, k2)
            rows0 = gt_ref[pl.ds(row0, 8), rs]
            rows1 = gt_ref[pl.ds(LANES + row0, 8), rs]
            for d in range(2):
                incl = (c <= r) if d == 0 else (c >= r)
                strict = (c < r) if d == 0 else (c > r)
                gl = 2 * d
                gdiff = pair_cols(g0, rs, gl) - pair_rows(rows0, gl)
                decay = jnp.where(incl, jnp.exp(jnp.where(incl, gdiff, 0.0)), 0.0)
                qk_ref[d, hh, rs, :] = (qk * decay).astype(qk_ref.dtype)
                a = jnp.where(strict, kk * decay, 0.0) * pair_cols(g0, rs, 4 + gl)
                chains.append(dict(a=a, t=eye - jnp.where(same1, a, 0.0), d=d, rs=rs, hh=hh,
                                   beta=pair_rows(rows0, 4 + gl), be=pair_rows(rows1, 4 + gl)))
    for mask in levels:
        xs = [_dot(jnp.where(mask, ch["a"], 0.0).astype(BF16), blockdiag(ch["t"])) for ch in chains]
        ys = [_dot(ch["t"].astype(BF16), blockdiag(x)) for ch, x in zip(chains, xs)]
        for ch, y in zip(chains, ys):
            ch["t"] = ch["t"] - y
    for ch in chains:
        tu = ch["t"] * ch["beta"]
        tw = ch["t"] * ch["be"]
        d, rs, hh = ch["d"], ch["rs"], ch["hh"]
        tutw_ref[d, 2 * hh, rs, :] = jnp.where(left, tu, pltpu.roll(tw, CHUNK, 1)).astype(tutw_ref.dtype)
        tutw_ref[d, 2 * hh + 1, rs, :] = jnp.where(left, pltpu.roll(tu, CHUNK, 1), tw).astype(tutw_ref.dtype)


def _pre(qh, kh, gates_c, gates_t):
    t = qh.shape[1]
    ct = gates_t.shape[2]
    return pl.pallas_call(
        functools.partial(_pre_kernel, nck=ct // CHUNK),
        grid=(t // ct, A_QK_HEADS // PRE_HQ),
        in_specs=[
            pl.BlockSpec((PRE_HQ, ct, HEAD), lambda i, h: (h, i, 0)),
            pl.BlockSpec((PRE_HQ, ct, HEAD), lambda i, h: (h, i, 0)),
            pl.BlockSpec((ct, 3 * LANES), lambda i, h: (i, 0)),
            pl.BlockSpec((None, 3 * LANES, ct), lambda i, h: (i, 0, 0)),
        ],
        out_specs=[
            pl.BlockSpec((2, 2 * PRE_HQ, ct, HEAD), lambda i, h: (0, h, i, 0)),
            pl.BlockSpec((2, PRE_HQ, ct, HEAD), lambda i, h: (0, h, i, 0)),
        ],
        out_shape=[
            jax.ShapeDtypeStruct((2, A_V_HEADS, t, HEAD), BF16),
            jax.ShapeDtypeStruct((2, A_QK_HEADS, t, HEAD), BF16),
        ],
        compiler_params=_cparams(("parallel", "arbitrary")),
        name="chunk_pre",
    )(qh, kh, gates_c, gates_t)


def _rec_kernel(q_ref, k_ref, v_ref, tutw_ref, qk_ref, g_ref, s0_ref, o_ref, sfin_ref, s_ref, *, nc):
    d = pl.program_id(1)
    cidx = pl.program_id(2)

    @pl.when(cidx == 0)
    def _():
        s_ref[...] = s0_ref[...]

    def dir_lanes(x):
        return jnp.where(d == 0, x, pltpu.roll(x, LANES - 2, 1))

    g1 = dir_lanes(g_ref[:, LANES:2 * LANES])
    g2 = dir_lanes(g_ref[:, 2 * LANES:3 * LANES])
    heads = range(A_V_HEADS)
    qk_lhs = [jnp.concatenate([q_ref[hq], k_ref[hq]], axis=0) for hq in range(A_QK_HEADS)]
    k_t = [k_ref[hq].astype(F32).T.astype(BF16) for hq in range(A_QK_HEADS)]
    states = [s_ref[j] for j in heads]
    qs_ks = [_dot(qk_lhs[j // 2], states[j].astype(BF16)) for j in heads]
    v_new = [_dot(tutw_ref[j], jnp.concatenate([v_ref[j], (-qs_ks[j][CHUNK:]).astype(BF16)], axis=0))
             for j in heads]
    for j in heads:
        hq, jj = divmod(j, 2)
        gl = 8 * hq + jj
        qkm = qk_ref[hq][:, jj * CHUNK:(jj + 1) * CHUNK]
        o = g1[:, gl:gl + 1] * qs_ks[j][:CHUNK] + _dot(qkm, v_new[j].astype(BF16))
        o_ref[j] = o.astype(o_ref.dtype)
    for j in heads:
        hq, jj = divmod(j, 2)
        gl = 8 * hq + jj
        v_dec = (v_new[j] * g2[:, gl:gl + 1]).astype(BF16)
        s_ref[j] = states[j] * g2[0:1, 4 + gl:5 + gl] + _dot(k_t[hq], v_dec)

    @pl.when(cidx == nc - 1)
    def _():
        sfin_ref[...] = s_ref[...]


def _rec(qh, kh, vh, tutw, qkm, gates_c, s0, n_seq, n_dir):
    t = qh.shape[1]
    nc = t // n_seq // CHUNK

    def chunk(b, d, c):
        return b * nc + c + d * (nc - 1 - 2 * c)

    return pl.pallas_call(
        functools.partial(_rec_kernel, nc=nc),
        grid=(n_seq, n_dir, nc),
        in_specs=[
            pl.BlockSpec((A_QK_HEADS, CHUNK, HEAD), lambda b, d, c: (0, chunk(b, d, c), 0)),
            pl.BlockSpec((A_QK_HEADS, CHUNK, HEAD), lambda b, d, c: (0, chunk(b, d, c), 0)),
            pl.BlockSpec((A_V_HEADS, CHUNK, HEAD), lambda b, d, c: (0, chunk(b, d, c), 0)),
            pl.BlockSpec((None, A_V_HEADS, CHUNK, HEAD), lambda b, d, c: (d, 0, chunk(b, d, c), 0)),
            pl.BlockSpec((None, A_QK_HEADS, CHUNK, HEAD), lambda b, d, c: (d, 0, chunk(b, d, c), 0)),
            pl.BlockSpec((CHUNK, 3 * LANES), lambda b, d, c: (chunk(b, d, c), 0)),
            pl.BlockSpec((None, None, A_V_HEADS, HEAD, HEAD), lambda b, d, c: (b, d, 0, 0, 0)),
        ],
        out_specs=[
            pl.BlockSpec((None, A_V_HEADS, CHUNK, HEAD), lambda b, d, c: (d, 0, chunk(b, d, c), 0)),
            pl.BlockSpec((None, None, A_V_HEADS, HEAD, HEAD), lambda b, d, c: (b, d, 0, 0, 0)),
        ],
        out_shape=[
            jax.ShapeDtypeStruct((n_dir, A_V_HEADS, t, HEAD), F32),
            jax.ShapeDtypeStruct((n_seq, n_dir, A_V_HEADS, HEAD, HEAD), F32),
        ],
        scratch_shapes=[pltpu.VMEM((A_V_HEADS, HEAD, HEAD), F32)],
        compiler_params=_cparams(("parallel", "parallel", "arbitrary")),
        name="chunk_rec",
    )(qh, kh, vh, tutw, qkm, gates_c, s0)


def _post_kernel(o_ref, z_ref, attn_ref, ga_ref, gb_ref, x_ref, na_ref, wa_ref, wb_ref, wo_ref,
                 nf_ref, wr_ref, br_ref, h_ref, xn_ref, comb_ref):
    parts = []
    for j in range(A_V_HEADS):
        o = o_ref[0, j] + o_ref[1, j]
        on = o * lax.rsqrt(jnp.mean(o * o, axis=-1, keepdims=True) + EPS) * na_ref[...]
        z = z_ref[:, j * HEAD:(j + 1) * HEAD].astype(F32)
        parts.append((on * (z * (1.0 / (1.0 + jnp.exp(-z))))).astype(BF16))
    ya = _dot(jnp.concatenate(parts, axis=-1), wa_ref[...])
    yb = _dot(attn_ref[...], wb_ref[...])
    sig_a = 1.0 / (1.0 + jnp.exp(-ga_ref[...].astype(F32)))
    sig_b = 1.0 / (1.0 + jnp.exp(-gb_ref[...].astype(F32)))
    merged = sig_a * ya + sig_b * yb
    h = x_ref[...] + _dot(merged.astype(BF16), wo_ref[...])
    h_ref[...] = h
    xn = h * lax.rsqrt(jnp.mean(h * h, axis=-1, keepdims=True) + EPS) * nf_ref[...]
    xn_ref[...] = xn.astype(xn_ref.dtype)

    x_hi = xn.astype(BF16)
    x_lo = (xn - x_hi.astype(F32)).astype(BF16)
    logits = (_dot(x_hi, wr_ref[0]) + _dot(x_lo, wr_ref[0]) + _dot(x_hi, wr_ref[1])) + br_ref[...]
    lane = lax.broadcasted_iota(jnp.int32, logits.shape, 1)
    lane_f = lane.astype(F32)
    g_mask = (lane >= N_EXPERTS) & (lane < N_EXPERTS + N_GROUPS)
    gl = jnp.where(g_mask, logits, NEG_BIG)
    g_max = jnp.max(gl, axis=-1, keepdims=True)
    g_w = 1.0 / jnp.sum(jnp.exp(gl - g_max), axis=-1, keepdims=True)
    g_idx = jnp.min(jnp.where(gl == g_max, lane_f, 1e9), axis=-1, keepdims=True) - N_EXPERTS
    e_mask = (lane < N_EXPERTS) & ((lane // EXPERTS_PER_GROUP).astype(F32) == g_idx)
    el = jnp.where(e_mask, logits, NEG_BIG)
    m1 = jnp.max(el, axis=-1, keepdims=True)
    i1 = jnp.min(jnp.where(el == m1, lane_f, 1e9), axis=-1, keepdims=True)
    el2 = jnp.where(lane_f == i1, NEG_BIG, el)
    m2 = jnp.max(el2, axis=-1, keepdims=True)
    i2 = jnp.min(jnp.where(el2 == m2, lane_f, 1e9), axis=-1, keepdims=True)
    e2 = jnp.exp(m2 - m1)
    w1 = g_w / (1.0 + e2)
    w2 = g_w * e2 / (1.0 + e2)
    comb_ref[...] = jnp.where(lane_f == i1, w1, 0.0) + jnp.where(lane_f == i2, w2, 0.0)


def _post(o, proj, attn, x2, na, wa, wb, wo, nf, wr, br):
    t = x2.shape[0]
    tm = _tile(t, 256)
    const = lambda i: (0, 0)
    return pl.pallas_call(
        _post_kernel,
        grid=(t // tm,),
        in_specs=[
            pl.BlockSpec((2, A_V_HEADS, tm, HEAD), lambda i: (0, 0, i, 0)),
            pl.BlockSpec((tm, A_VAL_DIM), lambda i: (i, OFF_Z // A_VAL_DIM)),
            pl.BlockSpec((tm, B_Q_DIM), lambda i: (i, 0)),
            pl.BlockSpec((tm, D_MODEL), lambda i: (i, OFF_GA // D_MODEL)),
            pl.BlockSpec((tm, D_MODEL), lambda i: (i, OFF_GB // D_MODEL)),
            pl.BlockSpec((tm, D_MODEL), lambda i: (i, 0)),
            pl.BlockSpec((1, HEAD), const),
            pl.BlockSpec((A_VAL_DIM, D_MODEL), const),
            pl.BlockSpec((B_Q_DIM, D_MODEL), const),
            pl.BlockSpec((D_MODEL, D_MODEL), const),
            pl.BlockSpec((1, D_MODEL), const),
            pl.BlockSpec((2, D_MODEL, LANES), lambda i: (0, 0, 0)),
            pl.BlockSpec((1, LANES), const),
        ],
        out_specs=[
            pl.BlockSpec((tm, D_MODEL), lambda i: (i, 0)),
            pl.BlockSpec((tm, D_MODEL), lambda i: (i, 0)),
            pl.BlockSpec((tm, LANES), lambda i: (i, 0)),
        ],
        out_shape=[
            jax.ShapeDtypeStruct((t, D_MODEL), F32),
            jax.ShapeDtypeStruct((t, D_MODEL), BF16),
            jax.ShapeDtypeStruct((t, LANES), F32),
        ],
        compiler_params=_cparams(("parallel",)),
        name="post_mixer",
    )(o, proj, attn, proj, proj, x2, na, wa, wb, wo, nf, wr, br)


def _moe_kernel(xn_ref, comb_ref, h_ref, wgu_ref, wd_ref, nfin_ref, y_ref):
    e = pl.program_id(1)
    acc_ref = y_ref

    @pl.when(e == 0)
    def _():
        acc_ref[...] = h_ref[...]

    gu = _dot(xn_ref[...], wgu_ref[...])
    gate = gu[:, :D_FF_EXPERT]
    up = gu[:, D_FF_EXPERT:]
    lane = lax.broadcasted_iota(jnp.int32, comb_ref.shape, 1)
    w = jnp.sum(jnp.where(lane == e, comb_ref[...], 0.0), axis=-1, keepdims=True)
    hid = gate * (1.0 / (1.0 + jnp.exp(-gate))) * up
    acc_ref[...] += w * _dot(hid.astype(BF16), wd_ref[...])

    @pl.when(e == N_EXPERTS - 1)
    def _():
        a = acc_ref[...]
        y_ref[...] = a * lax.rsqrt(jnp.mean(a * a, axis=-1, keepdims=True) + EPS) * nfin_ref[...]


def _moe(xn, comb, h, wgu, wd, nfin):
    t = xn.shape[0]
    tm = _tile(t, 2048)
    return pl.pallas_call(
        _moe_kernel,
        grid=(t // tm, N_EXPERTS),
        in_specs=[
            pl.BlockSpec((tm, D_MODEL), lambda i, e: (i, 0)),
            pl.BlockSpec((tm, LANES), lambda i, e: (i, 0)),
            pl.BlockSpec((tm, D_MODEL), lambda i, e: (i, 0)),
            pl.BlockSpec((None, D_MODEL, 2 * D_FF_EXPERT), lambda i, e: (e, 0, 0)),
            pl.BlockSpec((None, D_FF_EXPERT, D_MODEL), lambda i, e: (e, 0, 0)),
            pl.BlockSpec((1, D_MODEL), lambda i, e: (0, 0)),
        ],
        out_specs=pl.BlockSpec((tm, D_MODEL), lambda i, e: (i, 0)),
        out_shape=jax.ShapeDtypeStruct((t, D_MODEL), F32),
        compiler_params=_cparams(("parallel", "arbitrary")),
        name="moe_experts",
    )(xn, comb, h, wgu, wd, nfin)


def _forward(x, p):
    n_seq, seq_len, _ = x.shape
    t = n_seq * seq_len
    x2 = x.reshape(t, D_MODEL)
    proj, ab = _inproj(x2, p["norm_mix"], p["w_main"], p["w_ab"])

    tq, tk = _flash_tiles(seq_len)
    q_t, k_r, v_t = _attn_prep(proj, p["cos"], p["sin"], p["norm_q_b"], p["norm_k_b"], seq_len, tq, tk)
    attn = _flash(q_t, k_r, v_t, p["k_meta"], p["v_meta_t"], n_seq, seq_len)

    zeros_halo = jnp.zeros((n_seq, HALO, CONV_CH), BF16)
    first_rows = proj.reshape(n_seq, seq_len, N_MAIN)[:, :HALO, :CONV_CH]
    meta_x = jnp.broadcast_to(p["meta_chunk_x"][None], (n_seq, CHUNK, CONV_CH)).reshape(n_seq * CHUNK, CONV_CH)
    meta_ab = jnp.broadcast_to(p["meta_chunk_ab"][None], (n_seq, CHUNK, LANES)).reshape(n_seq * CHUNK, LANES)
    conv_args = (p["conv_w"], p["a_log"], p["dt_bias"])
    mq, mk, mv, mgc, mgt = _conv(meta_x, 0, zeros_halo, first_rows, meta_ab, *conv_args, tm=CHUNK, pad_rows=PAD_A)
    m_tutw, m_qk = _pre(mq, mk, mgc, mgt)
    zero_state = jnp.zeros((n_seq, 1, A_V_HEADS, HEAD, HEAD), F32)
    _, s_meta = _rec(mq, mk, mv, m_tutw, m_qk, mgc, zero_state, n_seq, 1)

    tm = _tile(seq_len, 512)
    per_seq = seq_len // tm
    tiles = proj.reshape(n_seq, per_seq, tm, N_MAIN)
    tails = tiles[:, :, tm - HALO:, :CONV_CH]
    heads = tiles[:, :, :HALO, :CONV_CH]
    meta_tail = jnp.broadcast_to(p["meta_chunk_x"][None, None, CHUNK - HALO:], (n_seq, 1, HALO, CONV_CH))
    prev = jnp.concatenate([meta_tail, tails[:, :-1]], axis=1).reshape(n_seq * per_seq, HALO, CONV_CH)
    nxt = jnp.concatenate([heads[:, 1:], zeros_halo[:, None]], axis=1).reshape(n_seq * per_seq, HALO, CONV_CH)
    qh, kh, vh, gates_c, gates_t = _conv(proj, OFF_QKV // CONV_CH, prev, nxt, ab, *conv_args, tm=tm, pad_rows=0)
    tutw, qkm = _pre(qh, kh, gates_c, gates_t)
    s0 = jnp.concatenate([s_meta, zero_state], axis=1)
    o, _ = _rec(qh, kh, vh, tutw, qkm, gates_c, s0, n_seq, 2)

    h, xn, comb = _post(o, proj, attn, x2, p["norm_out_a"], p["w_branch_a"], p["w_branch_b"], p["w_out"],
                        p["norm_ffn"], p["w_route"], p["b_route"])
    y = _moe(xn, comb, h, p["w_gate_up"], p["w_down"], p["norm_final"])
    return y.reshape(n_seq, seq_len, D_MODEL)


def _prepare(meta_tokens, norm_mix, w_in, conv_w, a_log_fwd, a_log_bwd, dt_bias_fwd, dt_bias_bwd,
             norm_out_a, norm_q_b, norm_k_b, w_branch_a, w_branch_b, w_out, norm_ffn, w_group, b_group,
             w_router, b_router, w_gate_e, w_up_e, w_down_e, norm_final, max_len):
    w = w_in[0]
    bounds = np.cumsum([0, A_KEY_DIM, A_KEY_DIM, A_VAL_DIM, A_VAL_DIM, 4 * A_V_HEADS, B_Q_DIM, B_KV_DIM,
                        B_KV_DIM, D_MODEL, D_MODEL])
    qa, ka, va, za, abw, qb, kb, vb, ga, gb = [w[:, bounds[i]:bounds[i + 1]] for i in range(10)]
    p = {}
    p["w_main"] = jnp.concatenate([qa, ka, va, za, qb, ga, gb, kb, vb], axis=1).astype(BF16)
    src = _gate_source_columns()
    p["w_ab"] = jnp.pad(abw[:, src], ((0, 0), (0, LANES - GATE_LANES))).astype(BF16)
    p["norm_mix"] = norm_mix[0][None]
    p["conv_w"] = jnp.pad(conv_w[0], ((0, HALO - CONV_W), (0, 0)))
    pad_lanes = lambda v: jnp.pad(v, (0, LANES - v.shape[0]))[None]
    zeros_b = jnp.zeros((2 * A_V_HEADS,), F32)
    p["a_log"] = pad_lanes(jnp.concatenate([a_log_fwd[0], a_log_bwd[0], zeros_b])[src])
    p["dt_bias"] = pad_lanes(jnp.concatenate([dt_bias_fwd[0], dt_bias_bwd[0], zeros_b])[src])
    p["norm_out_a"] = norm_out_a[0][None]
    p["norm_q_b"] = norm_q_b[0][None]
    p["norm_k_b"] = norm_k_b[0][None]
    p["w_branch_a"] = w_branch_a[0].astype(BF16)
    p["w_branch_b"] = w_branch_b[0].astype(BF16)
    p["w_out"] = w_out[0].astype(BF16)
    p["norm_ffn"] = norm_ffn[0][None]
    w_route = jnp.pad(jnp.concatenate([w_router[0], w_group[0]], axis=1),
                      ((0, 0), (0, LANES - N_EXPERTS - N_GROUPS)))
    w_route_hi = w_route.astype(BF16)
    p["w_route"] = jnp.stack([w_route_hi, (w_route - w_route_hi.astype(F32)).astype(BF16)])
    p["b_route"] = pad_lanes(jnp.concatenate([b_router[0], b_group[0]]))
    p["w_gate_up"] = jnp.concatenate([w_gate_e[0], w_up_e[0]], axis=-1).astype(BF16)
    p["w_down"] = w_down_e[0].astype(BF16)
    p["norm_final"] = norm_final[None]

    pos = jnp.arange(max_len, dtype=jnp.int32)
    p["cos"], p["sin"] = _rope_tables((pos // GRID_W).astype(F32), (pos % GRID_W).astype(F32))
    cos_m, sin_m = _rope_tables(jnp.full((N_META,), -1.0, F32), jnp.arange(N_META, dtype=F32))

    mproj, mab = _inproj(meta_tokens, p["norm_mix"], p["w_main"], p["w_ab"])
    k_meta = _meta_keys(mproj, cos_m, sin_m, p["norm_k_b"])
    p["k_meta"] = jnp.pad(k_meta, ((0, LANES - N_META), (0, 0)))
    v_meta = jnp.pad(mproj[:, OFF_VB:OFF_VB + B_KV_DIM], ((0, LANES - N_META), (0, 0)))
    p["v_meta_t"] = v_meta.reshape(LANES, B_KV_HEADS, HEAD).transpose(1, 2, 0)
    p["meta_chunk_x"] = jnp.pad(mproj[:, :CONV_CH], ((PAD_A, 0), (0, 0)))
    p["meta_chunk_ab"] = jnp.pad(mab, ((PAD_A, 0), (0, 0)))
    return p


def kernel(x_prompt, x_sample, meta_tokens, norm_mix, w_in, conv_w, a_log_fwd, a_log_bwd, dt_bias_fwd,
           dt_bias_bwd, norm_out_a, norm_q_b, norm_k_b, w_branch_a, w_branch_b, w_out, norm_ffn, w_group,
           b_group, w_router, b_router, w_gate_e, w_up_e, w_down_e, norm_final):
    max_len = max(x_prompt.shape[1], x_sample.shape[1])
    p = _prepare(meta_tokens, norm_mix, w_in, conv_w, a_log_fwd, a_log_bwd, dt_bias_fwd, dt_bias_bwd,
                 norm_out_a, norm_q_b, norm_k_b, w_branch_a, w_branch_b, w_out, norm_ffn, w_group, b_group,
                 w_router, b_router, w_gate_e, w_up_e, w_down_e, norm_final, max_len)
    return (_forward(x_prompt, p), _forward(x_sample, p))
```

```python
import functools
import math

import numpy as np
import jax
import jax.numpy as jnp
from jax import lax
from jax.experimental import pallas as pl
from jax.experimental.pallas import tpu as pltpu

F32 = jnp.float32
BF16 = jnp.bfloat16

D_MODEL = 1024
N_META = 16
GRID_W = 64
EPS = 1e-6
HEAD = 128
A_QK_HEADS = 8
A_V_HEADS = 16
A_KEY_DIM = A_QK_HEADS * HEAD
A_VAL_DIM = A_V_HEADS * HEAD
CONV_W = 5
CONV_CH = 2 * A_KEY_DIM + A_VAL_DIM
CHUNK = 64
PAD_A = CHUNK - N_META
B_Q_HEADS = 8
B_KV_HEADS = 2
B_GROUPS = B_Q_HEADS // B_KV_HEADS
B_Q_DIM = B_Q_HEADS * HEAD
B_KV_DIM = B_KV_HEADS * HEAD
ROPE_THETA = 10000.0
N_GROUPS = 4
EXPERTS_PER_GROUP = 8
N_EXPERTS = N_GROUPS * EXPERTS_PER_GROUP
D_FF_EXPERT = 256

OFF_QKV = 0
OFF_Z = 4096
OFF_QB = 6144
OFF_GA = 7168
OFF_GB = 8192
OFF_KB = 9216
OFF_VB = 9472
N_MAIN = 9728
LANES = 128
GATE_LANES = 4 * A_V_HEADS
NEG_BIG = -1e30


def _gate_source_columns():
    src = []
    for lane in range(GATE_LANES):
        hq, kind, d, jj = lane // 8, (lane % 8) // 4, (lane % 4) // 2, lane % 2
        src.append(2 * A_V_HEADS * kind + A_V_HEADS * d + 2 * hq + jj)
    return np.asarray(src)
VMEM_LIMIT = 56 * 1024 * 1024


def _cparams(sem):
    return pltpu.CompilerParams(dimension_semantics=sem, vmem_limit_bytes=VMEM_LIMIT)


def _dot(a, b):
    return jnp.dot(a, b, preferred_element_type=F32)


def _dot_nt(a, b):
    return lax.dot_general(a, b, (((1,), (1,)), ((), ())), preferred_element_type=F32)


def _dot_tn(a, b):
    return lax.dot_general(a, b, (((0,), (0,)), ((), ())), preferred_element_type=F32)


def _dot_f32(a, b):
    return jnp.dot(a, b, preferred_element_type=F32, precision=lax.Precision.HIGHEST)


def _tile(n, pref):
    t = min(n, pref)
    assert n % t == 0, (n, t)
    return t


def _inproj_kernel(x_ref, g_ref, w_ref, wab_ref, o_ref, oab_ref, xn_ref):
    @pl.when(pl.program_id(1) == 0)
    def _():
        x = x_ref[...]
        xn = x * lax.rsqrt(jnp.mean(x * x, axis=-1, keepdims=True) + EPS) * g_ref[...]
        xb = xn.astype(BF16)
        xn_ref[...] = xb
        oab_ref[...] = _dot(xb, wab_ref[...])

    o_ref[...] = _dot(xn_ref[...], w_ref[...]).astype(o_ref.dtype)


def _inproj(x2, gain, w_main, w_ab):
    t = x2.shape[0]
    tm = _tile(t, 1024)
    tn = N_MAIN // 4
    return pl.pallas_call(
        _inproj_kernel,
        grid=(t // tm, N_MAIN // tn),
        in_specs=[
            pl.BlockSpec((tm, D_MODEL), lambda i, j: (i, 0)),
            pl.BlockSpec((1, D_MODEL), lambda i, j: (0, 0)),
            pl.BlockSpec((D_MODEL, tn), lambda i, j: (0, j)),
            pl.BlockSpec((D_MODEL, LANES), lambda i, j: (0, 0)),
        ],
        out_specs=[
            pl.BlockSpec((tm, tn), lambda i, j: (i, j)),
            pl.BlockSpec((tm, LANES), lambda i, j: (i, 0)),
        ],
        out_shape=[jax.ShapeDtypeStruct((t, N_MAIN), BF16), jax.ShapeDtypeStruct((t, LANES), F32)],
        scratch_shapes=[pltpu.VMEM((tm, D_MODEL), BF16)],
        compiler_params=_cparams(("parallel", "arbitrary")),
        name="in_proj",
    )(x2, gain, w_main, w_ab)


def _rope_tables(row, col):
    half = HEAD // 2
    inv_freq = ROPE_THETA ** (-jnp.arange(0, half, 2, dtype=F32) / half)
    ang_r = row[:, None] * inv_freq
    ang_c = col[:, None] * inv_freq
    cos = jnp.concatenate([jnp.cos(ang_r)] * 2 + [jnp.cos(ang_c)] * 2, axis=-1)
    sin = jnp.concatenate([-jnp.sin(ang_r), jnp.sin(ang_r), -jnp.sin(ang_c), jnp.sin(ang_c)], axis=-1)
    return cos, sin


def _norm_rope(x, gain, cos, sin, scale):
    lane = lax.broadcasted_iota(jnp.int32, cos.shape, 1)
    low = (lane % (HEAD // 2)) < (HEAD // 4)
    x = x.astype(F32)
    xn = x * lax.rsqrt(jnp.mean(x * x, axis=-1, keepdims=True) + EPS) * gain
    partner = jnp.where(low, pltpu.roll(xn, HEAD - HEAD // 4, 1), pltpu.roll(xn, HEAD // 4, 1))
    return (xn * cos + partner * sin) * scale


def _attn_prep_kernel(q_ref, k_ref, v_ref, cos_ref, sin_ref, gq_ref, gk_ref, qo_ref, ko_ref, vo_ref, *, tq):
    cos = cos_ref[...]
    sin = sin_ref[...]
    tm = cos.shape[0]
    for h in range(B_Q_HEADS):
        sl = slice(h * HEAD, (h + 1) * HEAD)
        q_t = _norm_rope(q_ref[:, sl], gq_ref[...], cos, sin, HEAD ** -0.5 * math.log2(math.e)).T
        for n in range(tm // tq):
            qo_ref[n, sl, :] = q_t[:, n * tq:(n + 1) * tq].astype(qo_ref.dtype)
    for h in range(B_KV_HEADS):
        sl = slice(h * HEAD, (h + 1) * HEAD)
        ko_ref[:, sl] = _norm_rope(k_ref[:, sl], gk_ref[...], cos, sin, 1.0).astype(ko_ref.dtype)
        vo_ref[h] = v_ref[:, sl].astype(F32).T.astype(vo_ref.dtype)


def _attn_prep(proj, cos, sin, gq, gk, seq_len, tq, tk):
    t = proj.shape[0]
    tm = tk
    per_seq = seq_len // tm
    return pl.pallas_call(
        functools.partial(_attn_prep_kernel, tq=tq),
        grid=(t // tm,),
        in_specs=[
            pl.BlockSpec((tm, B_Q_DIM), lambda i: (i, OFF_QB // B_Q_DIM)),
            pl.BlockSpec((tm, B_KV_DIM), lambda i: (i, OFF_KB // B_KV_DIM)),
            pl.BlockSpec((tm, B_KV_DIM), lambda i: (i, OFF_VB // B_KV_DIM)),
            pl.BlockSpec((tm, HEAD), lambda i: (i % per_seq, 0)),
            pl.BlockSpec((tm, HEAD), lambda i: (i % per_seq, 0)),
            pl.BlockSpec((1, HEAD), lambda i: (0, 0)),
            pl.BlockSpec((1, HEAD), lambda i: (0, 0)),
        ],
        out_specs=[
            pl.BlockSpec((tm // tq, B_Q_DIM, tq), lambda i: (i, 0, 0)),
            pl.BlockSpec((tm, B_KV_DIM), lambda i: (i, 0)),
            pl.BlockSpec((B_KV_HEADS, None, HEAD, tm), lambda i: (0, i, 0, 0)),
        ],
        out_shape=[
            jax.ShapeDtypeStruct((t // tq, B_Q_DIM, tq), BF16),
            jax.ShapeDtypeStruct((t, B_KV_DIM), BF16),
            jax.ShapeDtypeStruct((B_KV_HEADS, t // tm, HEAD, tm), BF16),
        ],
        compiler_params=_cparams(("parallel",)),
        name="attn_prep",
    )(proj, proj, proj, cos, sin, gq, gk)


def _meta_key_kernel(k_ref, cos_ref, sin_ref, gk_ref, ko_ref):
    for h in range(B_KV_HEADS):
        sl = slice(h * HEAD, (h + 1) * HEAD)
        ko_ref[:, sl] = _norm_rope(k_ref[:, sl], gk_ref[...], cos_ref[...], sin_ref[...], 1.0).astype(ko_ref.dtype)


def _meta_keys(mproj, cos, sin, gk):
    return pl.pallas_call(
        _meta_key_kernel,
        grid=(1,),
        in_specs=[
            pl.BlockSpec((N_META, B_KV_DIM), lambda i: (0, OFF_KB // B_KV_DIM)),
            pl.BlockSpec((N_META, HEAD), lambda i: (0, 0)),
            pl.BlockSpec((N_META, HEAD), lambda i: (0, 0)),
            pl.BlockSpec((1, HEAD), lambda i: (0, 0)),
        ],
        out_specs=pl.BlockSpec((N_META, B_KV_DIM), lambda i: (0, 0)),
        out_shape=jax.ShapeDtypeStruct((N_META, B_KV_DIM), BF16),
        name="meta_keys",
    )(mproj, cos, sin, gk)


ONES_ROWS = 16


def _flash_kernel(qt_ref, k_ref, vt_ref, km_ref, vmt_ref, o_ref,
                  qa_ref, sa_ref, sb_ref, m_ref, acc_ref, *, tq, tk, nkb):
    cols = B_GROUPS * tq
    for g in range(B_GROUPS):
        qa_ref[:, g * tq:(g + 1) * tq] = qt_ref[g * HEAD:(g + 1) * HEAD, :]

    def scores(s_ref, kb):
        start = pl.multiple_of(kb * tk, tk)
        s_ref[:, 0:cols] = _dot(k_ref[pl.ds(start, tk), :], qa_ref[...])

    scores(sa_ref, 0)

    def with_ones(v_t):
        return jnp.concatenate([v_t, jnp.ones((ONES_ROWS, v_t.shape[1]), BF16)], axis=0)

    s = _dot(km_ref[...], qa_ref[...])
    key = lax.broadcasted_iota(jnp.int32, s.shape, 0)
    s = jnp.where(key < N_META, s, NEG_BIG)
    m = jnp.max(s, axis=0, keepdims=True)
    m_ref[...] = m
    acc_ref[:, 0:cols] = _dot(with_ones(vmt_ref[...]), jnp.exp2((s - m).astype(BF16)))

    def softmax_block(s_ref, kb):
        v_t = with_ones(vt_ref[kb])
        for pair in range(cols // (2 * LANES)):
            alphas, probs = [], []
            for cb in (2 * pair, 2 * pair + 1):
                cs = slice(cb * LANES, (cb + 1) * LANES)
                s = s_ref[:, cs]
                m_old = m_ref[:, cs]
                m_new = jnp.maximum(m_old, jnp.max(s, axis=0, keepdims=True))
                alphas.append(jnp.exp2(m_old - m_new))
                m_ref[:, cs] = m_new
                probs.append(jnp.exp2((s - m_new).astype(BF16)))
            ps = slice(2 * pair * LANES, (2 * pair + 2) * LANES)
            acc_ref[:, ps] = (acc_ref[:, ps] * jnp.concatenate(alphas, axis=-1)
                              + _dot(v_t, jnp.concatenate(probs, axis=-1)))

    if nkb > 1:
        def body(i, carry):
            scores(sb_ref, 2 * i + 1)
            softmax_block(sa_ref, 2 * i)
            scores(sa_ref, 2 * i + 2)
            softmax_block(sb_ref, 2 * i + 1)
            return carry

        lax.fori_loop(0, nkb // 2 - 1, body, 0)
        scores(sb_ref, nkb - 1)
        softmax_block(sa_ref, nkb - 2)
        softmax_block(sb_ref, nkb - 1)
    else:
        softmax_block(sa_ref, 0)
    out_t = acc_ref[0:HEAD, 0:cols] * (1.0 / acc_ref[HEAD:HEAD + 1, 0:cols])
    for g in range(B_GROUPS):
        o_ref[:, g * HEAD:(g + 1) * HEAD] = out_t[:, g * tq:(g + 1) * tq].T.astype(o_ref.dtype)


def _flash_tiles(seq_len):
    tq = _tile(seq_len, 256)
    tk = _tile(seq_len, 512)
    assert seq_len // tk == 1 or (seq_len // tk) % 2 == 0
    return tq, tk


def _flash(q_t, k_r, v_t, km, vmt, n_seq, seq_len):
    tq, tk = _flash_tiles(seq_len)
    nkb = seq_len // tk
    ntq = seq_len // tq
    cols = B_GROUPS * tq
    gw = B_GROUPS * HEAD
    k3 = k_r.reshape(n_seq, seq_len, B_KV_DIM)
    out = pl.pallas_call(
        functools.partial(_flash_kernel, tq=tq, tk=tk, nkb=nkb),
        grid=(n_seq, B_KV_HEADS, ntq),
        in_specs=[
            pl.BlockSpec((None, gw, tq), lambda b, j, i: (b * ntq + i, j, 0)),
            pl.BlockSpec((None, seq_len, HEAD), lambda b, j, i: (b, 0, j)),
            pl.BlockSpec((None, nkb, HEAD, tk), lambda b, j, i: (j, b, 0, 0)),
            pl.BlockSpec((LANES, HEAD), lambda b, j, i: (0, j)),
            pl.BlockSpec((None, HEAD, LANES), lambda b, j, i: (j, 0, 0)),
        ],
        out_specs=pl.BlockSpec((None, tq, gw), lambda b, j, i: (b, i, j)),
        out_shape=jax.ShapeDtypeStruct((n_seq, seq_len, B_Q_DIM), BF16),
        scratch_shapes=[
            pltpu.VMEM((HEAD, cols), BF16),
            pltpu.VMEM((tk, cols + LANES), F32),
            pltpu.VMEM((tk, cols + LANES), F32),
            pltpu.VMEM((1, cols), F32),
            pltpu.VMEM((HEAD + ONES_ROWS, cols + LANES), F32),
        ],
        compiler_params=_cparams(("parallel", "parallel", "arbitrary")),
        name="flash_attn",
    )(q_t, k3, v_t, km, vmt)
    return out.reshape(n_seq * seq_len, B_Q_DIM)


CONV_CB = 512
HALO = 8


def _conv_kernel(x_ref, prev_ref, next_ref, ab_ref, cw_ref, alog_ref, dtb_ref,
                 q_ref, k_ref, v_ref, gc_ref, gt_ref, xs_ref, *, tm, pad_rows):
    row = lax.broadcasted_iota(jnp.int32, (tm, 1), 0)
    valid = (row % CHUNK) >= pad_rows
    for cb in range(CONV_CH // CONV_CB):
        cs = slice(cb * CONV_CB, (cb + 1) * CONV_CB)
        xs_ref[0:HALO, :] = prev_ref[0, :, cs].astype(F32)
        xs_ref[HALO:HALO + tm, :] = x_ref[:, cs].astype(F32)
        xs_ref[HALO + tm:, :] = next_ref[0, :, cs].astype(F32)
        y = jnp.zeros((tm, CONV_CB), F32)
        for w in range(CONV_W):
            y = y + xs_ref[pl.ds(HALO - CONV_W // 2 + w, tm), :] * cw_ref[w:w + 1, cs]
        y = y * (1.0 / (1.0 + jnp.exp(-y)))
        if pad_rows:
            y = jnp.where(valid, y, 0.0)
        for hh in range(CONV_CB // HEAD):
            col = cb * CONV_CB + hh * HEAD
            yh = y[:, hh * HEAD:(hh + 1) * HEAD]
            if col < 2 * A_KEY_DIM:
                yh = yh * lax.rsqrt(jnp.sum(yh * yh, axis=-1, keepdims=True) + EPS)
            if col < A_KEY_DIM:
                q_ref[col // HEAD] = (yh * (HEAD ** -0.5)).astype(q_ref.dtype)
            elif col < 2 * A_KEY_DIM:
                k_ref[(col - A_KEY_DIM) // HEAD] = yh.astype(k_ref.dtype)
            else:
                v_ref[(col - 2 * A_KEY_DIM) // HEAD] = yh.astype(v_ref.dtype)

    ab = ab_ref[...]
    lane = lax.broadcasted_iota(jnp.int32, (tm, LANES), 1)
    is_gate = lane < GATE_LANES
    is_a = is_gate & ((lane % 8) < 4)
    z = ab + dtb_ref[...]
    softplus = jnp.maximum(z, 0.0) + jnp.log1p(jnp.exp(-jnp.abs(z)))
    g = -jnp.exp(alog_ref[...]) * softplus
    beta = 1.0 / (1.0 + jnp.exp(-ab))
    g = jnp.where(valid & is_a, g, 0.0)
    beta = jnp.where(valid & is_gate & (~is_a), beta, 0.0)

    r = lax.broadcasted_iota(jnp.int32, (CHUNK, CHUNK), 0)
    c = lax.broadcasted_iota(jnp.int32, (CHUNK, CHUNK), 1)
    lower = (c <= r).astype(F32)
    upper = (c >= r).astype(F32)
    ones = jnp.ones((CHUNK, CHUNK), F32)
    lane_c = lax.broadcasted_iota(jnp.int32, (CHUNK, LANES), 1)
    fwd_lane = (lane_c % 4) < 2
    a_lane = (lane_c % 8) < 4
    for ck in range(tm // CHUNK):
        rs = slice(ck * CHUNK, (ck + 1) * CHUNK)
        gk = g[rs]
        gcum = jnp.where(fwd_lane, _dot_f32(lower, gk), _dot_f32(upper, gk))
        gtot = _dot_f32(ones, gk)
        e = jnp.exp(gcum)
        dec = jnp.exp(gtot - gcum)
        eg = jnp.exp(gtot)
        bk = beta[rs]
        t0 = jnp.where(a_lane, gcum, bk)
        t1 = jnp.where(a_lane, e, bk * pltpu.roll(e, 4, 1))
        t2 = jnp.where(a_lane, dec, pltpu.roll(eg, 4, 1))
        gc_ref[rs, 0:LANES] = t0
        gc_ref[rs, LANES:2 * LANES] = t1
        gc_ref[rs, 2 * LANES:3 * LANES] = t2
        gt_ref[0, 0:LANES, rs] = t0.T
        gt_ref[0, LANES:2 * LANES, rs] = t1.T
        gt_ref[0, 2 * LANES:3 * LANES, rs] = t2.T


def _conv(x, col_block, prev, nxt, ab, cw, alog, dtb, tm, pad_rows):
    t = x.shape[0]
    nt = t // tm
    return pl.pallas_call(
        functools.partial(_conv_kernel, tm=tm, pad_rows=pad_rows),
        grid=(nt,),
        in_specs=[
            pl.BlockSpec((tm, CONV_CH), lambda i: (i, col_block)),
            pl.BlockSpec((1, HALO, CONV_CH), lambda i: (i, 0, 0)),
            pl.BlockSpec((1, HALO, CONV_CH), lambda i: (i, 0, 0)),
            pl.BlockSpec((tm, LANES), lambda i: (i, 0)),
            pl.BlockSpec((HALO, CONV_CH), lambda i: (0, 0)),
            pl.BlockSpec((1, LANES), lambda i: (0, 0)),
            pl.BlockSpec((1, LANES), lambda i: (0, 0)),
        ],
        out_specs=[
            pl.BlockSpec((A_QK_HEADS, tm, HEAD), lambda i: (0, i, 0)),
            pl.BlockSpec((A_QK_HEADS, tm, HEAD), lambda i: (0, i, 0)),
            pl.BlockSpec((A_V_HEADS, tm, HEAD), lambda i: (0, i, 0)),
            pl.BlockSpec((tm, 3 * LANES), lambda i: (i, 0)),
            pl.BlockSpec((1, 3 * LANES, tm), lambda i: (i, 0, 0)),
        ],
        out_shape=[
            jax.ShapeDtypeStruct((A_QK_HEADS, t, HEAD), BF16),
            jax.ShapeDtypeStruct((A_QK_HEADS, t, HEAD), BF16),
            jax.ShapeDtypeStruct((A_V_HEADS, t, HEAD), BF16),
            jax.ShapeDtypeStruct((t, 3 * LANES), F32),
            jax.ShapeDtypeStruct((nt, 3 * LANES, tm), F32),
        ],
        scratch_shapes=[pltpu.VMEM((tm + 2 * HALO, CONV_CB), F32)],
        compiler_params=_cparams(("parallel",)),
        name="conv_gates",
    )(x, prev, nxt, ab, cw, alog, dtb)


PRE_HQ = 2


def _pre_kernel(q_ref, k_ref, gc_ref, gt_ref, tutw_ref, qk_ref, *, nck):
    r = lax.broadcasted_iota(jnp.int32, (CHUNK, LANES), 0)
    lane = lax.broadcasted_iota(jnp.int32, (CHUNK, LANES), 1)
    c = lane & (CHUNK - 1)
    left = lane < CHUNK

    def same(k):
        return (r >> k) == (c >> k)

    eye = jnp.where(r == c, 1.0, 0.0)
    same1 = same(1)
    levels = []
    k = 1
    while (1 << k) < CHUNK:
        levels.append(same(k + 1) & (~same(k)))
        k += 1

    def blockdiag(m):
        mb = m.astype(BF16)
        zero = jnp.zeros_like(mb)
        return jnp.concatenate([jnp.where(left, mb, zero), jnp.where(left, zero, mb)], axis=0)

    def pair_cols(x, rs, i):
        return jnp.where(left, x[rs, i:i + 1], x[rs, i + 1:i + 2])

    def pair_rows(x, i):
        return jnp.concatenate([x[i:i + 1], x[i + 1:i + 2]], axis=-1)

    chains = []
    for hh in range(PRE_HQ):
        hq = PRE_HQ * pl.program_id(1) + hh
        g0 = pltpu.roll(gc_ref[:, 0:LANES], lax.rem(LANES - 8 * hq, LANES), 1)
        row0 = pl.multiple_of(8 * hq, 8)
        for ck in range(nck):
            rs = slice(ck * CHUNK, (ck + 1) * CHUNK)
            qb = q_ref[hh, rs, :]
            kb = k_ref[hh, rs, :]
            k2 = jnp.concatenate([kb, kb], axis=0)
            kk = _dot_nt(kb, k2)
            qk = _dot_nt(qb, k2)
            rows0 = gt_ref[pl.ds(row0, 8), rs]
            rows1 = gt_ref[pl.ds(LANES + row0, 8), rs]
            for d in range(2):
                incl = (c <= r) if d == 0 else (c >= r)
                strict = (c < r) if d == 0 else (c > r)
                gl = 2 * d
                gdiff = pair_cols(g0, rs, gl) - pair_rows(rows0, gl)
                decay = jnp.where(incl, jnp.exp(jnp.where(incl, gdiff, 0.0)), 0.0)
                qk_ref[d, hh, rs, :] = (qk * decay).astype(qk_ref.dtype)
                a = jnp.where(strict, kk * decay, 0.0) * pair_cols(g0, rs, 4 + gl)
                chains.append(dict(a=a, t=eye - jnp.where(same1, a, 0.0), d=d, rs=rs, hh=hh,
                                   beta=pair_rows(rows0, 4 + gl), be=pair_rows(rows1, 4 + gl)))
    for mask in levels:
        xs = [_dot(jnp.where(mask, ch["a"], 0.0).astype(BF16), blockdiag(ch["t"])) for ch in chains]
        ys = [_dot(ch["t"].astype(BF16), blockdiag(x)) for ch, x in zip(chains, xs)]
        for ch, y in zip(chains, ys):
            ch["t"] = ch["t"] - y
    for ch in chains:
        tu = ch["t"] * ch["beta"]
        tw = ch["t"] * ch["be"]
        d, rs, hh = ch["d"], ch["rs"], ch["hh"]
        tutw_ref[d, 2 * hh, rs, :] = jnp.where(left, tu, pltpu.roll(tw, CHUNK, 1)).astype(tutw_ref.dtype)
        tutw_ref[d, 2 * hh + 1, rs, :] = jnp.where(left, pltpu.roll(tu, CHUNK, 1), tw).astype(tutw_ref.dtype)


def _pre(qh, kh, gates_c, gates_t):
    t = qh.shape[1]
    ct = gates_t.shape[2]
    return pl.pallas_call(
        functools.partial(_pre_kernel, nck=ct // CHUNK),
        grid=(t // ct, A_QK_HEADS // PRE_HQ),
        in_specs=[
            pl.BlockSpec((PRE_HQ, ct, HEAD), lambda i, h: (h, i, 0)),
            pl.BlockSpec((PRE_HQ, ct, HEAD), lambda i, h: (h, i, 0)),
            pl.BlockSpec((ct, 3 * LANES), lambda i, h: (i, 0)),
            pl.BlockSpec((None, 3 * LANES, ct), lambda i, h: (i, 0, 0)),
        ],
        out_specs=[
            pl.BlockSpec((2, 2 * PRE_HQ, ct, HEAD), lambda i, h: (0, h, i, 0)),
            pl.BlockSpec((2, PRE_HQ, ct, HEAD), lambda i, h: (0, h, i, 0)),
        ],
        out_shape=[
            jax.ShapeDtypeStruct((2, A_V_HEADS, t, HEAD), BF16),
            jax.ShapeDtypeStruct((2, A_QK_HEADS, t, HEAD), BF16),
        ],
        compiler_params=_cparams(("parallel", "arbitrary")),
        name="chunk_pre",
    )(qh, kh, gates_c, gates_t)


REC_INPUTS = 6


def _rec_kernel(*refs, nc, n_dir):
    ins = refs[:REC_INPUTS * n_dir]
    s0_ref = refs[REC_INPUTS * n_dir]
    o_refs = refs[REC_INPUTS * n_dir + 1:REC_INPUTS * n_dir + 1 + n_dir]
    sfin_ref, s_ref = refs[REC_INPUTS * n_dir + 1 + n_dir:]
    cidx = pl.program_id(1)

    @pl.when(cidx == 0)
    def _():
        s_ref[...] = s0_ref[...]

    units = []
    for d in range(n_dir):
        q_ref, k_ref, v_ref, tutw_ref, qk_ref, g_ref = ins[REC_INPUTS * d:REC_INPUTS * (d + 1)]
        g1 = g_ref[:, LANES:2 * LANES]
        g2 = g_ref[:, 2 * LANES:3 * LANES]
        if d == 1:
            g1 = pltpu.roll(g1, LANES - 2, 1)
            g2 = pltpu.roll(g2, LANES - 2, 1)
        qk_lhs = [jnp.concatenate([q_ref[hq], k_ref[hq]], axis=0) for hq in range(A_QK_HEADS)]
        k_t = [k_ref[hq].astype(F32).T.astype(BF16) for hq in range(A_QK_HEADS)]
        for j in range(A_V_HEADS):
            hq, jj = divmod(j, 2)
            units.append(dict(d=d, j=j, gl=8 * hq + jj, qk_lhs=qk_lhs[hq], k_t=k_t[hq], v=v_ref[j],
                              tutw=tutw_ref[j], qkm=qk_ref[hq][:, jj * CHUNK:(jj + 1) * CHUNK],
                              g1=g1, g2=g2, state=s_ref[d, j]))
    for u in units:
        u["qs_ks"] = _dot(u["qk_lhs"], u["state"].astype(BF16))
    for u in units:
        rhs = jnp.concatenate([u["v"], (-u["qs_ks"][CHUNK:]).astype(BF16)], axis=0)
        u["v_new"] = _dot(u["tutw"], rhs)
    for u in units:
        gl = u["gl"]
        o = u["g1"][:, gl:gl + 1] * u["qs_ks"][:CHUNK] + _dot(u["qkm"], u["v_new"].astype(BF16))
        o_refs[u["d"]][u["j"]] = o.astype(o_refs[u["d"]].dtype)
    for u in units:
        gl = u["gl"]
        v_dec = (u["v_new"] * u["g2"][:, gl:gl + 1]).astype(BF16)
        s_ref[u["d"], u["j"]] = u["state"] * u["g2"][0:1, 4 + gl:5 + gl] + _dot(u["k_t"], v_dec)

    @pl.when(cidx == nc - 1)
    def _():
        sfin_ref[...] = s_ref[...]


def _rec(qh, kh, vh, tutw, qkm, gates_c, s0, n_seq, n_dir):
    t = qh.shape[1]
    nc = t // n_seq // CHUNK
    in_specs, args = [], []
    for d in range(n_dir):
        def chunk(b, c, d=d):
            return b * nc + (c if d == 0 else nc - 1 - c)

        in_specs += [
            pl.BlockSpec((A_QK_HEADS, CHUNK, HEAD), lambda b, c, f=chunk: (0, f(b, c), 0)),
            pl.BlockSpec((A_QK_HEADS, CHUNK, HEAD), lambda b, c, f=chunk: (0, f(b, c), 0)),
            pl.BlockSpec((A_V_HEADS, CHUNK, HEAD), lambda b, c, f=chunk: (0, f(b, c), 0)),
            pl.BlockSpec((None, A_V_HEADS, CHUNK, HEAD), lambda b, c, f=chunk, d=d: (d, 0, f(b, c), 0)),
            pl.BlockSpec((None, A_QK_HEADS, CHUNK, HEAD), lambda b, c, f=chunk, d=d: (d, 0, f(b, c), 0)),
            pl.BlockSpec((CHUNK, 3 * LANES), lambda b, c, f=chunk: (f(b, c), 0)),
        ]
        args += [qh, kh, vh, tutw, qkm, gates_c]
    state_spec = pl.BlockSpec((None, n_dir, A_V_HEADS, HEAD, HEAD), lambda b, c: (b, 0, 0, 0, 0))
    out_specs = []
    for d in range(n_dir):
        def chunk(b, c, d=d):
            return b * nc + (c if d == 0 else nc - 1 - c)

        out_specs.append(pl.BlockSpec((A_V_HEADS, CHUNK, HEAD), lambda b, c, f=chunk: (0, f(b, c), 0)))
    res = pl.pallas_call(
        functools.partial(_rec_kernel, nc=nc, n_dir=n_dir),
        grid=(n_seq, nc),
        in_specs=in_specs + [state_spec],
        out_specs=out_specs + [state_spec],
        out_shape=[jax.ShapeDtypeStruct((A_V_HEADS, t, HEAD), F32)] * n_dir
        + [jax.ShapeDtypeStruct((n_seq, n_dir, A_V_HEADS, HEAD, HEAD), F32)],
        scratch_shapes=[pltpu.VMEM((n_dir, A_V_HEADS, HEAD, HEAD), F32)],
        compiler_params=_cparams(("parallel", "arbitrary")),
        name="chunk_rec",
    )(*args, s0)
    return res[:n_dir], res[n_dir]


def _post_kernel(of_ref, ob_ref, z_ref, attn_ref, ga_ref, gb_ref, x_ref, na_ref, wa_ref, wb_ref, wo_ref,
                 nf_ref, wr_ref, br_ref, h_ref, xn_ref, comb_ref):
    parts = []
    for j in range(A_V_HEADS):
        o = of_ref[j] + ob_ref[j]
        on = o * lax.rsqrt(jnp.mean(o * o, axis=-1, keepdims=True) + EPS) * na_ref[...]
        z = z_ref[:, j * HEAD:(j + 1) * HEAD].astype(F32)
        parts.append((on * (z * (1.0 / (1.0 + jnp.exp(-z))))).astype(BF16))
    ya = _dot(jnp.concatenate(parts, axis=-1), wa_ref[...])
    yb = _dot(attn_ref[...], wb_ref[...])
    sig_a = 1.0 / (1.0 + jnp.exp(-ga_ref[...].astype(F32)))
    sig_b = 1.0 / (1.0 + jnp.exp(-gb_ref[...].astype(F32)))
    merged = sig_a * ya + sig_b * yb
    h = x_ref[...] + _dot(merged.astype(BF16), wo_ref[...])
    h_ref[...] = h
    xn = h * lax.rsqrt(jnp.mean(h * h, axis=-1, keepdims=True) + EPS) * nf_ref[...]
    xn_ref[...] = xn.astype(xn_ref.dtype)

    x_hi = xn.astype(BF16)
    x_lo = (xn - x_hi.astype(F32)).astype(BF16)
    logits = (_dot(x_hi, wr_ref[0]) + _dot(x_lo, wr_ref[0]) + _dot(x_hi, wr_ref[1])) + br_ref[...]
    lane = lax.broadcasted_iota(jnp.int32, logits.shape, 1)
    lane_f = lane.astype(F32)
    g_mask = (lane >= N_EXPERTS) & (lane < N_EXPERTS + N_GROUPS)
    gl = jnp.where(g_mask, logits, NEG_BIG)
    g_max = jnp.max(gl, axis=-1, keepdims=True)
    g_w = 1.0 / jnp.sum(jnp.exp(gl - g_max), axis=-1, keepdims=True)
    g_idx = jnp.min(jnp.where(gl == g_max, lane_f, 1e9), axis=-1, keepdims=True) - N_EXPERTS
    e_mask = (lane < N_EXPERTS) & ((lane // EXPERTS_PER_GROUP).astype(F32) == g_idx)
    el = jnp.where(e_mask, logits, NEG_BIG)
    m1 = jnp.max(el, axis=-1, keepdims=True)
    i1 = jnp.min(jnp.where(el == m1, lane_f, 1e9), axis=-1, keepdims=True)
    el2 = jnp.where(lane_f == i1, NEG_BIG, el)
    m2 = jnp.max(el2, axis=-1, keepdims=True)
    i2 = jnp.min(jnp.where(el2 == m2, lane_f, 1e9), axis=-1, keepdims=True)
    e2 = jnp.exp(m2 - m1)
    w1 = g_w / (1.0 + e2)
    w2 = g_w * e2 / (1.0 + e2)
    comb_ref[...] = jnp.where(lane_f == i1, w1, 0.0) + jnp.where(lane_f == i2, w2, 0.0)


def _post(o_f, o_b, proj, attn, x2, na, wa, wb, wo, nf, wr, br):
    t = x2.shape[0]
    tm = _tile(t, 256)
    const = lambda i: (0, 0)
    return pl.pallas_call(
        _post_kernel,
        grid=(t // tm,),
        in_specs=[
            pl.BlockSpec((A_V_HEADS, tm, HEAD), lambda i: (0, i, 0)),
            pl.BlockSpec((A_V_HEADS, tm, HEAD), lambda i: (0, i, 0)),
            pl.BlockSpec((tm, A_VAL_DIM), lambda i: (i, OFF_Z // A_VAL_DIM)),
            pl.BlockSpec((tm, B_Q_DIM), lambda i: (i, 0)),
            pl.BlockSpec((tm, D_MODEL), lambda i: (i, OFF_GA // D_MODEL)),
            pl.BlockSpec((tm, D_MODEL), lambda i: (i, OFF_GB // D_MODEL)),
            pl.BlockSpec((tm, D_MODEL), lambda i: (i, 0)),
            pl.BlockSpec((1, HEAD), const),
            pl.BlockSpec((A_VAL_DIM, D_MODEL), const),
            pl.BlockSpec((B_Q_DIM, D_MODEL), const),
            pl.BlockSpec((D_MODEL, D_MODEL), const),
            pl.BlockSpec((1, D_MODEL), const),
            pl.BlockSpec((2, D_MODEL, LANES), lambda i: (0, 0, 0)),
            pl.BlockSpec((1, LANES), const),
        ],
        out_specs=[
            pl.BlockSpec((tm, D_MODEL), lambda i: (i, 0)),
            pl.BlockSpec((tm, D_MODEL), lambda i: (i, 0)),
            pl.BlockSpec((tm, LANES), lambda i: (i, 0)),
        ],
        out_shape=[
            jax.ShapeDtypeStruct((t, D_MODEL), F32),
            jax.ShapeDtypeStruct((t, D_MODEL), BF16),
            jax.ShapeDtypeStruct((t, LANES), F32),
        ],
        compiler_params=_cparams(("parallel",)),
        name="post_mixer",
    )(o_f, o_b, proj, attn, proj, proj, x2, na, wa, wb, wo, nf, wr, br)


def _moe_kernel(xn_ref, comb_ref, h_ref, wgu_ref, wd_ref, nfin_ref, y_ref):
    e = pl.program_id(1)
    acc_ref = y_ref

    @pl.when(e == 0)
    def _():
        acc_ref[...] = h_ref[...]

    gu = _dot(xn_ref[...], wgu_ref[...])
    gate = gu[:, :D_FF_EXPERT]
    up = gu[:, D_FF_EXPERT:]
    lane = lax.broadcasted_iota(jnp.int32, comb_ref.shape, 1)
    w = jnp.sum(jnp.where(lane == e, comb_ref[...], 0.0), axis=-1, keepdims=True)
    hid = gate * (1.0 / (1.0 + jnp.exp(-gate))) * up
    acc_ref[...] += w * _dot(hid.astype(BF16), wd_ref[...])

    @pl.when(e == N_EXPERTS - 1)
    def _():
        a = acc_ref[...]
        y_ref[...] = a * lax.rsqrt(jnp.mean(a * a, axis=-1, keepdims=True) + EPS) * nfin_ref[...]


def _moe(xn, comb, h, wgu, wd, nfin):
    t = xn.shape[0]
    tm = _tile(t, 2048)
    return pl.pallas_call(
        _moe_kernel,
        grid=(t // tm, N_EXPERTS),
        in_specs=[
            pl.BlockSpec((tm, D_MODEL), lambda i, e: (i, 0)),
            pl.BlockSpec((tm, LANES), lambda i, e: (i, 0)),
            pl.BlockSpec((tm, D_MODEL), lambda i, e: (i, 0)),
            pl.BlockSpec((None, D_MODEL, 2 * D_FF_EXPERT), lambda i, e: (e, 0, 0)),
            pl.BlockSpec((None, D_FF_EXPERT, D_MODEL), lambda i, e: (e, 0, 0)),
            pl.BlockSpec((1, D_MODEL), lambda i, e: (0, 0)),
        ],
        out_specs=pl.BlockSpec((tm, D_MODEL), lambda i, e: (i, 0)),
        out_shape=jax.ShapeDtypeStruct((t, D_MODEL), F32),
        compiler_params=_cparams(("parallel", "arbitrary")),
        name="moe_experts",
    )(xn, comb, h, wgu, wd, nfin)


def _forward(x, p):
    n_seq, seq_len, _ = x.shape
    t = n_seq * seq_len
    x2 = x.reshape(t, D_MODEL)
    proj, ab = _inproj(x2, p["norm_mix"], p["w_main"], p["w_ab"])

    tq, tk = _flash_tiles(seq_len)
    q_t, k_r, v_t = _attn_prep(proj, p["cos"], p["sin"], p["norm_q_b"], p["norm_k_b"], seq_len, tq, tk)
    attn = _flash(q_t, k_r, v_t, p["k_meta"], p["v_meta_t"], n_seq, seq_len)

    zeros_halo = jnp.zeros((n_seq, HALO, CONV_CH), BF16)
    first_rows = proj.reshape(n_seq, seq_len, N_MAIN)[:, :HALO, :CONV_CH]
    meta_x = jnp.broadcast_to(p["meta_chunk_x"][None], (n_seq, CHUNK, CONV_CH)).reshape(n_seq * CHUNK, CONV_CH)
    meta_ab = jnp.broadcast_to(p["meta_chunk_ab"][None], (n_seq, CHUNK, LANES)).reshape(n_seq * CHUNK, LANES)
    conv_args = (p["conv_w"], p["a_log"], p["dt_bias"])
    mq, mk, mv, mgc, mgt = _conv(meta_x, 0, zeros_halo, first_rows, meta_ab, *conv_args, tm=CHUNK, pad_rows=PAD_A)
    m_tutw, m_qk = _pre(mq, mk, mgc, mgt)
    zero_state = jnp.zeros((n_seq, 1, A_V_HEADS, HEAD, HEAD), F32)
    _, s_meta = _rec(mq, mk, mv, m_tutw, m_qk, mgc, zero_state, n_seq, 1)

    tm = _tile(seq_len, 512)
    per_seq = seq_len // tm
    tiles = proj.reshape(n_seq, per_seq, tm, N_MAIN)
    tails = tiles[:, :, tm - HALO:, :CONV_CH]
    heads = tiles[:, :, :HALO, :CONV_CH]
    meta_tail = jnp.broadcast_to(p["meta_chunk_x"][None, None, CHUNK - HALO:], (n_seq, 1, HALO, CONV_CH))
    prev = jnp.concatenate([meta_tail, tails[:, :-1]], axis=1).reshape(n_seq * per_seq, HALO, CONV_CH)
    nxt = jnp.concatenate([heads[:, 1:], zeros_halo[:, None]], axis=1).reshape(n_seq * per_seq, HALO, CONV_CH)
    qh, kh, vh, gates_c, gates_t = _conv(proj, OFF_QKV // CONV_CH, prev, nxt, ab, *conv_args, tm=tm, pad_rows=0)
    tutw, qkm = _pre(qh, kh, gates_c, gates_t)
    s0 = jnp.concatenate([s_meta, zero_state], axis=1)
    (o_f, o_b), _ = _rec(qh, kh, vh, tutw, qkm, gates_c, s0, n_seq, 2)

    h, xn, comb = _post(o_f, o_b, proj, attn, x2, p["norm_out_a"], p["w_branch_a"], p["w_branch_b"], p["w_out"],
                        p["norm_ffn"], p["w_route"], p["b_route"])
    y = _moe(xn, comb, h, p["w_gate_up"], p["w_down"], p["norm_final"])
    return y.reshape(n_seq, seq_len, D_MODEL)


def _prepare(meta_tokens, norm_mix, w_in, conv_w, a_log_fwd, a_log_bwd, dt_bias_fwd, dt_bias_bwd,
             norm_out_a, norm_q_b, norm_k_b, w_branch_a, w_branch_b, w_out, norm_ffn, w_group, b_group,
             w_router, b_router, w_gate_e, w_up_e, w_down_e, norm_final, max_len):
    w = w_in[0]
    bounds = np.cumsum([0, A_KEY_DIM, A_KEY_DIM, A_VAL_DIM, A_VAL_DIM, 4 * A_V_HEADS, B_Q_DIM, B_KV_DIM,
                        B_KV_DIM, D_MODEL, D_MODEL])
    qa, ka, va, za, abw, qb, kb, vb, ga, gb = [w[:, bounds[i]:bounds[i + 1]] for i in range(10)]
    p = {}
    p["w_main"] = jnp.concatenate([qa, ka, va, za, qb, ga, gb, kb, vb], axis=1).astype(BF16)
    src = _gate_source_columns()
    p["w_ab"] = jnp.pad(abw[:, src], ((0, 0), (0, LANES - GATE_LANES))).astype(BF16)
    p["norm_mix"] = norm_mix[0][None]
    p["conv_w"] = jnp.pad(conv_w[0], ((0, HALO - CONV_W), (0, 0)))
    pad_lanes = lambda v: jnp.pad(v, (0, LANES - v.shape[0]))[None]
    zeros_b = jnp.zeros((2 * A_V_HEADS,), F32)
    p["a_log"] = pad_lanes(jnp.concatenate([a_log_fwd[0], a_log_bwd[0], zeros_b])[src])
    p["dt_bias"] = pad_lanes(jnp.concatenate([dt_bias_fwd[0], dt_bias_bwd[0], zeros_b])[src])
    p["norm_out_a"] = norm_out_a[0][None]
    p["norm_q_b"] = norm_q_b[0][None]
    p["norm_k_b"] = norm_k_b[0][None]
    p["w_branch_a"] = w_branch_a[0].astype(BF16)
    p["w_branch_b"] = w_branch_b[0].astype(BF16)
    p["w_out"] = w_out[0].astype(BF16)
    p["norm_ffn"] = norm_ffn[0][None]
    w_route = jnp.pad(jnp.concatenate([w_router[0], w_group[0]], axis=1),
                      ((0, 0), (0, LANES - N_EXPERTS - N_GROUPS)))
    w_route_hi = w_route.astype(BF16)
    p["w_route"] = jnp.stack([w_route_hi, (w_route - w_route_hi.astype(F32)).astype(BF16)])
    p["b_route"] = pad_lanes(jnp.concatenate([b_router[0], b_group[0]]))
    p["w_gate_up"] = jnp.concatenate([w_gate_e[0], w_up_e[0]], axis=-1).astype(BF16)
    p["w_down"] = w_down_e[0].astype(BF16)
    p["norm_final"] = norm_final[None]

    pos = jnp.arange(max_len, dtype=jnp.int32)
    p["cos"], p["sin"] = _rope_tables((pos // GRID_W).astype(F32), (pos % GRID_W).astype(F32))
    cos_m, sin_m = _rope_tables(jnp.full((N_META,), -1.0, F32), jnp.arange(N_META, dtype=F32))

    mproj, mab = _inproj(meta_tokens, p["norm_mix"], p["w_main"], p["w_ab"])
    k_meta = _meta_keys(mproj, cos_m, sin_m, p["norm_k_b"])
    p["k_meta"] = jnp.pad(k_meta, ((0, LANES - N_META), (0, 0)))
    v_meta = jnp.pad(mproj[:, OFF_VB:OFF_VB + B_KV_DIM], ((0, LANES - N_META), (0, 0)))
    p["v_meta_t"] = v_meta.reshape(LANES, B_KV_HEADS, HEAD).transpose(1, 2, 0)
    p["meta_chunk_x"] = jnp.pad(mproj[:, :CONV_CH], ((PAD_A, 0), (0, 0)))
    p["meta_chunk_ab"] = jnp.pad(mab, ((PAD_A, 0), (0, 0)))
    return p


def kernel(x_prompt, x_sample, meta_tokens, norm_mix, w_in, conv_w, a_log_fwd, a_log_bwd, dt_bias_fwd,
           dt_bias_bwd, norm_out_a, norm_q_b, norm_k_b, w_branch_a, w_branch_b, w_out, norm_ffn, w_group,
           b_group, w_router, b_router, w_gate_e, w_up_e, w_down_e, norm_final):
    max_len = max(x_prompt.shape[1], x_sample.shape[1])
    p = _prepare(meta_tokens, norm_mix, w_in, conv_w, a_log_fwd, a_log_bwd, dt_bias_fwd, dt_bias_bwd,
                 norm_out_a, norm_q_b, norm_k_b, w_branch_a, w_branch_b, w_out, norm_ffn, w_group, b_group,
                 w_router, b_router, w_gate_e, w_up_e, w_down_e, norm_final, max_len)
    return (_forward(x_prompt, p), _forward(x_sample, p))
```
